```python
import jax, jax.numpy as jnp
from jax import lax
import numpy as np

D_MODEL = 1024
BATCH = 8
SEQ = 4096
DEPTH = 4

N_MIXERS = 2
N_LAYERS_A = (DEPTH + 1) // 2
N_LAYERS_B = DEPTH // 2
MIX_WIDTH = D_MODEL
CONV_WIDTH = 3
CONV_GROUPS = 8
POOL_WINDOWS = (2, 4, 8, 16)
N_POOL_GROUPS = len(POOL_WINDOWS)
POOL_GROUP_DIM = MIX_WIDTH // N_POOL_GROUPS
PLE_DIM = 256
EPS = 1e-6

kernel_name = "hybrid_shortconv_pool_ple_trunk"


def rmsnorm(x, g):
    xf = x.astype(jnp.float32)
    r = lax.rsqrt(jnp.mean(xf * xf, axis=-1, keepdims=True) + EPS)
    return (xf * r).astype(x.dtype) * g


def causal_conv3(u, w):
    s = u.shape[1]
    up = jnp.pad(u, ((0, 0), (CONV_WIDTH - 1, 0), (0, 0)))
    return up[:, 0:s] * w[0] + up[:, 1:s + 1] * w[1] + up[:, 2:s + 2] * w[2]


def short_conv_mixer(h, w_in, w_conv, w_out):
    proj = h @ w_in
    b_g, c_g, v, z = jnp.split(proj, 4, axis=-1)
    y = b_g * causal_conv3(c_g * v, w_conv)
    return (jax.nn.silu(z) * y) @ w_out


def causal_window_mean(u, window):
    s = u.shape[1]
    cs = jnp.cumsum(u.astype(jnp.float32), axis=1)
    csp = jnp.pad(cs, ((0, 0), (window, 0), (0, 0)))
    win_sum = csp[:, window:window + s] - csp[:, 0:s]
    count = jnp.minimum(jnp.arange(1, s + 1, dtype=jnp.float32), float(window))
    return (win_sum / count[None, :, None]).astype(u.dtype)


def pool_mixer(h, w_in, w_grp, scale, w_out):
    proj = h @ w_in
    u, z = jnp.split(proj, 2, axis=-1)
    bsz, s, _ = u.shape
    u4 = u.reshape(bsz, s, N_POOL_GROUPS, POOL_GROUP_DIM)
    pooled = jnp.stack([causal_window_mean(u4[:, :, g], w) for g, w in enumerate(POOL_WINDOWS)], axis=2)
    d = pooled - u4
    mixed = jnp.einsum('bsgc,gcd->bsgd', d, w_grp).reshape(bsz, s, MIX_WIDTH) * scale
    return (jax.nn.silu(z) * mixed) @ w_out


def setup_inputs(seed: int = 0) -> dict:
    key = jax.random.key(seed)
    ks = jax.random.split(key, 16)
    f32 = jnp.float32
    E, D, G = MIX_WIDTH, D_MODEL, POOL_GROUP_DIM
    nrm = lambda k, shape, fan_in: jax.random.normal(k, shape, f32) * (fan_in ** -0.5)
    gain = lambda k, shape: 1.0 + 0.02 * jax.random.normal(k, shape, f32)
    return {
        "x": jax.random.normal(ks[0], (BATCH, SEQ, D), f32),
        "p": jax.random.normal(ks[1], (DEPTH, BATCH, SEQ, PLE_DIM), f32),
        "norm_mix": gain(ks[2], (DEPTH, D)),
        "a_w_in": nrm(ks[3], (N_LAYERS_A, D, 4 * E), D),
        "a_w_conv": nrm(ks[4], (N_LAYERS_A, CONV_WIDTH, E), CONV_WIDTH),
        "a_w_out": nrm(ks[5], (N_LAYERS_A, E, D), E),
        "b_w_in": nrm(ks[6], (N_LAYERS_B, D, 2 * E), D),
        "b_w_grp": nrm(ks[7], (N_LAYERS_B, N_POOL_GROUPS, G, G), G),
        "b_scale": gain(ks[8], (N_LAYERS_B, E)),
        "b_w_out": nrm(ks[9], (N_LAYERS_B, E, D), E),
        "ple_norm": gain(ks[10], (DEPTH, D)),
        "ple_w_gate": nrm(ks[11], (DEPTH, D, D), D),
        "ple_w_proj": nrm(ks[12], (DEPTH, PLE_DIM, D), PLE_DIM),
        "final_norm": gain(ks[13], (D,)),
    }


def reference(x, p, norm_mix, a_w_in, a_w_conv, a_w_out, b_w_in, b_w_grp, b_scale, b_w_out,
              ple_norm, ple_w_gate, ple_w_proj, final_norm):
    h = x
    for i in range(DEPTH):
        hn = rmsnorm(h, norm_mix[i])
        j = i // N_MIXERS
        if i % N_MIXERS == 0:
            h = h + short_conv_mixer(hn, a_w_in[j], a_w_conv[j], a_w_out[j])
        else:
            h = h + pool_mixer(hn, b_w_in[j], b_w_grp[j], b_scale[j], b_w_out[j])
        gate = jax.nn.sigmoid(rmsnorm(h, ple_norm[i]) @ ple_w_gate[i])
        h = h + gate * (p[i] @ ple_w_proj[i])
    return rmsnorm(h, final_norm)
```

```python
import functools

import jax
import jax.numpy as jnp
from jax import lax
from jax.experimental import pallas as pl
from jax.experimental.pallas import tpu as pltpu

EPS = 1e-6
CONV_WIDTH = 3
POOL_WINDOWS = (2, 4, 8, 16)
HIST = 16
CB = 256
TM = 512
VMEM_LIMIT_BYTES = 56 * 1024 * 1024

F32 = jnp.float32
BF16 = jnp.bfloat16


def _rmsnorm(x, g):
    ms = jnp.mean(x * x, axis=-1, keepdims=True)
    return (x * lax.rsqrt(ms + EPS)) * g


def _shift_rows(ext, k):
    return pltpu.roll(ext, k, axis=0)[HIST:, :]


def _ple_and_store(h, p_ref, pn_ref, wg_ref, wp_ref, fn_ref, o_ref):
    hn = _rmsnorm(h, pn_ref[...]).astype(BF16)
    gate = jax.nn.sigmoid(jnp.dot(hn, wg_ref[...], preferred_element_type=F32))
    pp = jnp.dot(p_ref[...].astype(BF16), wp_ref[...], preferred_element_type=F32)
    h = h + gate * pp
    if fn_ref is not None:
        h = _rmsnorm(h, fn_ref[...])
    o_ref[...] = h


def _conv_layer_kernel(final_norm, h_ref, p_ref, nm_ref, w_in_ref, w_conv_ref, w_out_ref,
                       pn_ref, wg_ref, wp_ref, *rest):
    if final_norm:
        fn_ref, o_ref, hist_ref = rest
    else:
        fn_ref = None
        o_ref, hist_ref = rest

    @pl.when(pl.program_id(1) == 0)
    def _():
        hist_ref[...] = jnp.zeros_like(hist_ref)

    h = h_ref[...]
    tm, e = h.shape[0], w_out_ref.shape[0]
    hn = _rmsnorm(h, nm_ref[...]).astype(BF16)
    acc = h
    for j in range(e // CB):
        cs = slice(j * CB, (j + 1) * CB)
        proj = jnp.dot(hn, w_in_ref[:, j * 4 * CB:(j + 1) * 4 * CB], preferred_element_type=F32)
        b_g, c_g, v, z = (proj[:, k * CB:(k + 1) * CB] for k in range(4))
        u = c_g * v
        ext = jnp.concatenate([hist_ref[:, cs], u], axis=0)
        hist_ref[:, cs] = u[tm - HIST:, :]
        w = w_conv_ref[:, cs]
        y = _shift_rows(ext, 2) * w[0:1] + _shift_rows(ext, 1) * w[1:2] + u * w[2:3]
        act = (z * jax.nn.sigmoid(z)) * (b_g * y)
        acc = acc + jnp.dot(act.astype(BF16), w_out_ref[cs, :], preferred_element_type=F32)
    _ple_and_store(acc, p_ref, pn_ref, wg_ref, wp_ref, fn_ref, o_ref)


def _pool_layer_kernel(final_norm, h_ref, p_ref, nm_ref, w_in_ref, w_grp_ref, scale_ref, w_out_ref,
                       pn_ref, wg_ref, wp_ref, *rest):
    if final_norm:
        fn_ref, o_ref, hist_ref = rest
    else:
        fn_ref = None
        o_ref, hist_ref = rest

    s = pl.program_id(1)

    @pl.when(s == 0)
    def _():
        hist_ref[...] = jnp.zeros_like(hist_ref)

    h = h_ref[...]
    tm, e = h.shape[0], w_out_ref.shape[0]
    hn = _rmsnorm(h, nm_ref[...]).astype(BF16)
    t1 = (lax.broadcasted_iota(jnp.int32, (tm, CB), 0) + (s * tm + 1)).astype(F32)
    acc = h
    for j, window in enumerate(POOL_WINDOWS):
        cs = slice(j * CB, (j + 1) * CB)
        proj = jnp.dot(hn, w_in_ref[:, j * 2 * CB:(j + 1) * 2 * CB], preferred_element_type=F32)
        u, z = proj[:, :CB], proj[:, CB:]
        ext = jnp.concatenate([hist_ref[:, cs], u], axis=0)
        hist_ref[:, cs] = u[tm - HIST:, :]
        span = 1
        while span < window:
            ext = ext + pltpu.roll(ext, span, axis=0)
            span *= 2
        inv_count = 1.0 / jnp.minimum(t1, float(window))
        d = ext[HIST:, :] * inv_count - u
        mixed = jnp.dot(d.astype(BF16), w_grp_ref[j], preferred_element_type=F32) * scale_ref[:, cs]
        act = (z * jax.nn.sigmoid(z)) * mixed
        acc = acc + jnp.dot(act.astype(BF16), w_out_ref[cs, :], preferred_element_type=F32)
    _ple_and_store(acc, p_ref, pn_ref, wg_ref, wp_ref, fn_ref, o_ref)


def _const_spec(shape):
    zeros = (0,) * len(shape)
    return pl.BlockSpec(shape, lambda b, s: zeros, pipeline_mode=pl.Buffered(1))


def _layer_call(kernel_fn, h, p, layer, mixer_args, ple_args, final_norm, name):
    batch, seq, d = h.shape
    ple_dim = p.shape[-1]
    e = d
    args = [h, p, *mixer_args, *ple_args]
    in_specs = [
        pl.BlockSpec((None, TM, d), lambda b, s: (b, s, 0)),
        pl.BlockSpec((None, None, TM, ple_dim), lambda b, s: (layer, b, s, 0)),
    ] + [_const_spec(a.shape) for a in args[2:]]
    if final_norm is not None:
        args.append(final_norm)
        in_specs.append(_const_spec(final_norm.shape))
    return pl.pallas_call(
        functools.partial(kernel_fn, final_norm is not None),
        grid=(batch, seq // TM),
        in_specs=in_specs,
        out_specs=pl.BlockSpec((None, TM, d), lambda b, s: (b, s, 0)),
        out_shape=jax.ShapeDtypeStruct(h.shape, h.dtype),
        scratch_shapes=[pltpu.VMEM((HIST, e), F32)],
        compiler_params=pltpu.CompilerParams(
            dimension_semantics=("arbitrary", "arbitrary"),
            vmem_limit_bytes=VMEM_LIMIT_BYTES,
        ),
        name=name,
    )(*args)


def _block_columns(w, parts):
    d, pe = w.shape
    e = pe // parts
    return w.reshape(d, parts, e // CB, CB).transpose(0, 2, 1, 3).reshape(d, pe)


def kernel(x, p, norm_mix, a_w_in, a_w_conv, a_w_out, b_w_in, b_w_grp, b_scale, b_w_out, ple_norm,
           ple_w_gate, ple_w_proj, final_norm):
    depth = p.shape[0]
    assert x.shape[1] % TM == 0 and TM >= HIST and max(POOL_WINDOWS) - 1 <= HIST
    row = lambda v: v.reshape(1, -1)
    h = x
    for i in range(depth):
        j = i // 2
        ple_args = (row(ple_norm[i]), ple_w_gate[i].astype(BF16), ple_w_proj[i].astype(BF16))
        fn = row(final_norm) if i == depth - 1 else None
        if i % 2 == 0:
            mixer_args = (row(norm_mix[i]), _block_columns(a_w_in[j], 4).astype(BF16), a_w_conv[j],
                          a_w_out[j].astype(BF16))
            h = _layer_call(_conv_layer_kernel, h, p, i, mixer_args, ple_args, fn, f"conv_layer_{i}")
        else:
            mixer_args = (row(norm_mix[i]), _block_columns(b_w_in[j], 2).astype(BF16),
                          b_w_grp[j].astype(BF16), row(b_scale[j]), b_w_out[j].astype(BF16))
            h = _layer_call(_pool_layer_kernel, h, p, i, mixer_args, ple_args, fn, f"pool_layer_{i}")
    return h
```

```python
import functools

import jax
import jax.numpy as jnp
from jax import lax
from jax.experimental import pallas as pl
from jax.experimental.pallas import tpu as pltpu

EPS = 1e-6
POOL_WINDOWS = (2, 4, 8, 16)
HIST = 16
CB = 256
TM = 512
SUB = 256
STAGE_ROWS = 256
STAGE_COLS = 1024
VMEM_LIMIT_BYTES = 56 * 1024 * 1024

F32 = jnp.float32
BF16 = jnp.bfloat16


def _rmsnorm(x, g):
    ms = jnp.mean(x * x, axis=-1, keepdims=True)
    return (x * lax.rsqrt(ms + EPS)) * g


def _dot(a, b):
    return jnp.dot(a, b, preferred_element_type=F32)


def _shift_rows(ext, k):
    return pltpu.roll(ext, k, axis=0)[HIST:, :]


def _weight_chunks(src, dst, parts=1):
    k, n = src.shape
    e = n // parts
    chunks = []
    for r0 in range(0, k, STAGE_ROWS):
        nr = min(STAGE_ROWS, k - r0)
        for c0 in range(0, n, STAGE_COLS):
            nc = min(STAGE_COLS, n - c0)

            def store(x, r0=r0, nr=nr, c0=c0, nc=nc):
                if parts == 1:
                    dst[r0:r0 + nr, c0:c0 + nc] = x
                    return
                for b0 in range(0, nc, CB):
                    part, j = divmod(c0 + b0, e)
                    d0 = (j // CB * parts + part) * CB
                    dst[r0:r0 + nr, d0:d0 + CB] = x[:, b0:b0 + CB]

            chunks.append((src.at[pl.ds(r0, nr), pl.ds(c0, nc)], nr, nc, store))
    return chunks


def _load_weights(chunks, stage_ref, sem_ref):
    def copy(i):
        src, nr, nc, _ = chunks[i]
        slot = i % 2
        return pltpu.make_async_copy(src, stage_ref.at[slot, pl.ds(0, nr), pl.ds(0, nc)], sem_ref.at[slot])

    copy(0).start()
    for i, (_, nr, nc, store) in enumerate(chunks):
        if i + 1 < len(chunks):
            copy(i + 1).start()
        copy(i).wait()
        store(stage_ref[i % 2, 0:nr, 0:nc].astype(BF16))


def _first_step():
    return jnp.logical_and(pl.program_id(0) == 0, pl.program_id(1) == 0)


def _ple_finish(h, pp, hn, wg_s, fn_ref):
    gate = jax.nn.sigmoid(_dot(hn, wg_s[...]))
    h = h + gate * pp
    if fn_ref is not None:
        h = _rmsnorm(h, fn_ref[...])
    return h


def _run_tile(h_ref, p_ref, nm_ref, pn_ref, fn_ref, o_ref, act_ref, w_in_s, w_out_s, wg_s, wp_s,
              n_parts, mixer_block):
    n_sub = TM // SUB
    e = w_out_s.shape[0]
    n_blk = e // CB
    bw = n_parts * CB
    prev = None
    for k in range(n_sub + 1):
        rows = slice(k * SUB, (k + 1) * SUB)
        if k < n_sub:
            h = h_ref[rows, :]
            hn = _rmsnorm(h, nm_ref[...]).astype(BF16)
        for j in range(n_blk):
            if k < n_sub:
                proj = _dot(hn, w_in_s[:, j * bw:(j + 1) * bw])
            if prev is not None:
                if j == 0:
                    prev["h"] = prev["h"] + _dot(act_ref[prev["rows"], :], w_out_s[...])
                elif j == 1:
                    prev["pp"] = _dot(p_ref[prev["rows"], :].astype(BF16), wp_s[...])
                    prev["hn"] = _rmsnorm(prev["h"], pn_ref[...]).astype(BF16)
                elif j == 2:
                    o_ref[prev["rows"], :] = _ple_finish(prev["h"], prev["pp"], prev["hn"], wg_s, fn_ref)
            if k < n_sub:
                act_ref[rows, j * CB:(j + 1) * CB] = mixer_block(k, j, proj).astype(BF16)
        prev = {"h": h, "rows": rows} if k < n_sub else None


def _conv_layer_kernel(final_norm, widx, h_ref, p_ref, nm_ref, w_conv_ref, pn_ref, *rest):
    if final_norm:
        fn_ref, *rest = rest
    else:
        fn_ref = None
    (w_in_hbm, w_out_hbm, wg_hbm, wp_hbm, o_ref,
     w_in_s, w_out_s, wg_s, wp_s, act_ref, hist_ref, stage_ref, sem_ref) = rest
    w_in_hbm, w_out_hbm, wg_hbm, wp_hbm = (
        r.at[i] for r, i in zip((w_in_hbm, w_out_hbm, wg_hbm, wp_hbm), widx))

    @pl.when(_first_step())
    def _():
        chunks = (_weight_chunks(w_in_hbm, w_in_s, parts=4) + _weight_chunks(w_out_hbm, w_out_s)
                  + _weight_chunks(wg_hbm, wg_s) + _weight_chunks(wp_hbm, wp_s))
        _load_weights(chunks, stage_ref, sem_ref)

    @pl.when(pl.program_id(1) == 0)
    def _():
        hist_ref[...] = jnp.zeros_like(hist_ref)

    def mixer_block(k, j, proj):
        cs = slice(j * CB, (j + 1) * CB)
        b_g, c_g, v, z = (proj[:, q * CB:(q + 1) * CB] for q in range(4))
        u = c_g * v
        ext = jnp.concatenate([hist_ref[:, cs], u], axis=0)
        hist_ref[:, cs] = u[SUB - HIST:, :]
        w = w_conv_ref[:, cs]
        y = _shift_rows(ext, 2) * w[0:1] + _shift_rows(ext, 1) * w[1:2] + u * w[2:3]
        return (z * jax.nn.sigmoid(z)) * (b_g * y)

    _run_tile(h_ref, p_ref, nm_ref, pn_ref, fn_ref, o_ref, act_ref, w_in_s, w_out_s, wg_s, wp_s,
              4, mixer_block)


def _pool_layer_kernel(final_norm, widx, h_ref, p_ref, nm_ref, scale_ref, pn_ref, *rest):
    if final_norm:
        fn_ref, *rest = rest
    else:
        fn_ref = None
    (w_in_hbm, w_grp_hbm, w_out_hbm, wg_hbm, wp_hbm, o_ref,
     w_in_s, w_grp_s, w_out_s, wg_s, wp_s, act_ref, hist_ref, stage_ref, sem_ref) = rest
    w_in_hbm, w_grp_hbm, w_out_hbm, wg_hbm, wp_hbm = (
        r.at[i] for r, i in zip((w_in_hbm, w_grp_hbm, w_out_hbm, wg_hbm, wp_hbm), widx))

    @pl.when(_first_step())
    def _():
        chunks = _weight_chunks(w_in_hbm, w_in_s, parts=2)
        for g in range(len(POOL_WINDOWS)):
            chunks += _weight_chunks(w_grp_hbm.at[g], w_grp_s.at[g])
        chunks += (_weight_chunks(w_out_hbm, w_out_s) + _weight_chunks(wg_hbm, wg_s)
                   + _weight_chunks(wp_hbm, wp_s))
        _load_weights(chunks, stage_ref, sem_ref)

    s = pl.program_id(1)

    @pl.when(s == 0)
    def _():
        hist_ref[...] = jnp.zeros_like(hist_ref)

    row = lax.broadcasted_iota(jnp.int32, (SUB, CB), 0)

    def mixer_block(k, j, proj):
        cs = slice(j * CB, (j + 1) * CB)
        window = POOL_WINDOWS[j]
        u, z = proj[:, :CB], proj[:, CB:]
        ext = jnp.concatenate([hist_ref[:, cs], u], axis=0)
        hist_ref[:, cs] = u[SUB - HIST:, :]
        span = 1
        while span < window:
            ext = ext + pltpu.roll(ext, span, axis=0)
            span *= 2
        t1 = (row + (s * TM + k * SUB + 1)).astype(F32)
        d = ext[HIST:, :] * (1.0 / jnp.minimum(t1, float(window))) - u
        mixed = _dot(d.astype(BF16), w_grp_s[j]) * scale_ref[:, cs]
        return (z * jax.nn.sigmoid(z)) * mixed

    _run_tile(h_ref, p_ref, nm_ref, pn_ref, fn_ref, o_ref, act_ref, w_in_s, w_out_s, wg_s, wp_s,
              2, mixer_block)


def _layer_call(kernel_fn, h, p, layer, small_args, final_norm, weights, name):
    batch, seq, d = h.shape
    widx = tuple(i for _, i in weights)
    weights = [w for w, _ in weights]
    ple_dim = p.shape[-1]
    small = list(small_args) + ([final_norm] if final_norm is not None else [])
    const = lambda a: pl.BlockSpec(a.shape, lambda b, s: (0,) * a.ndim)
    in_specs = (
        [pl.BlockSpec((None, TM, d), lambda b, s: (b, s, 0)),
         pl.BlockSpec((None, None, TM, ple_dim), lambda b, s: (layer, b, s, 0))]
        + [const(a) for a in small]
        + [pl.BlockSpec(memory_space=pl.ANY)] * len(weights)
    )
    scratch = [pltpu.VMEM(w.shape[1:], BF16) for w in weights] + [
        pltpu.VMEM((TM, d), BF16),
        pltpu.VMEM((HIST, d), F32),
        pltpu.VMEM((2, STAGE_ROWS, STAGE_COLS), F32),
        pltpu.SemaphoreType.DMA((2,)),
    ]
    return pl.pallas_call(
        functools.partial(kernel_fn, final_norm is not None, widx),
        grid=(batch, seq // TM),
        in_specs=in_specs,
        out_specs=pl.BlockSpec((None, TM, d), lambda b, s: (b, s, 0)),
        out_shape=jax.ShapeDtypeStruct(h.shape, h.dtype),
        scratch_shapes=scratch,
        compiler_params=pltpu.CompilerParams(
            dimension_semantics=("arbitrary", "arbitrary"),
            vmem_limit_bytes=VMEM_LIMIT_BYTES,
        ),
        name=name,
    )(h, p, *small, *weights)


def kernel(x, p, norm_mix, a_w_in, a_w_conv, a_w_out, b_w_in, b_w_grp, b_scale, b_w_out, ple_norm,
           ple_w_gate, ple_w_proj, final_norm):
    depth = p.shape[0]
    assert x.shape[1] % TM == 0 and TM % SUB == 0 and SUB >= HIST >= max(POOL_WINDOWS) - 1
    row = lambda v: v.reshape(1, -1)
    h = x
    for i in range(depth):
        j = i // 2
        fn = row(final_norm) if i == depth - 1 else None
        if i % 2 == 0:
            h = _layer_call(_conv_layer_kernel, h, p, i,
                            (row(norm_mix[i]), a_w_conv[j], row(ple_norm[i])), fn,
                            ((a_w_in, j), (a_w_out, j), (ple_w_gate, i), (ple_w_proj, i)),
                            f"conv_layer_{i}")
        else:
            h = _layer_call(_pool_layer_kernel, h, p, i,
                            (row(norm_mix[i]), row(b_scale[j]), row(ple_norm[i])), fn,
                            ((b_w_in, j), (b_w_grp, j), (b_w_out, j), (ple_w_gate, i), (ple_w_proj, i)),
                            f"pool_layer_{i}")
    return h
```

```python
import functools

import jax
import jax.numpy as jnp
from jax import lax
from jax.experimental import pallas as pl
from jax.experimental.pallas import tpu as pltpu

EPS = 1e-6
POOL_WINDOWS = (2, 4, 8, 16)
HIST = 16
CB = 256
TM = 1024
SUB = 256
STAGE_ROWS = 256
STAGE_COLS = 1024
VMEM_LIMIT_BYTES = 56 * 1024 * 1024

F32 = jnp.float32
BF16 = jnp.bfloat16


def _rmsnorm(x, g):
    ms = jnp.mean(x * x, axis=-1, keepdims=True)
    return (x * lax.rsqrt(ms + EPS)) * g


def _dot(a, b):
    return jnp.dot(a, b, preferred_element_type=F32)


def _shift_rows(ext, k):
    return pltpu.roll(ext, k, axis=0)[HIST:, :]


def _weight_chunks(src, dst, parts=1):
    k, n = src.shape
    e = n // parts
    chunks = []
    for r0 in range(0, k, STAGE_ROWS):
        nr = min(STAGE_ROWS, k - r0)
        for c0 in range(0, n, STAGE_COLS):
            nc = min(STAGE_COLS, n - c0)

            def store(x, r0=r0, nr=nr, c0=c0, nc=nc):
                if parts == 1:
                    dst[r0:r0 + nr, c0:c0 + nc] = x
                    return
                for b0 in range(0, nc, CB):
                    part, j = divmod(c0 + b0, e)
                    d0 = (j // CB * parts + part) * CB
                    dst[r0:r0 + nr, d0:d0 + CB] = x[:, b0:b0 + CB]

            chunks.append((src.at[pl.ds(r0, nr), pl.ds(c0, nc)], nr, nc, store))
    return chunks


def _load_weights(chunks, stage_ref, sem_ref):
    def copy(i):
        src, nr, nc, _ = chunks[i]
        slot = i % 2
        return pltpu.make_async_copy(src, stage_ref.at[slot, pl.ds(0, nr), pl.ds(0, nc)], sem_ref.at[slot])

    copy(0).start()
    for i, (_, nr, nc, store) in enumerate(chunks):
        if i + 1 < len(chunks):
            copy(i + 1).start()
        copy(i).wait()
        store(stage_ref[i % 2, 0:nr, 0:nc].astype(BF16))


def _first_step():
    return jnp.logical_and(pl.program_id(0) == 0, pl.program_id(1) == 0)


def _ple_finish(h, pp, hn, wg_s, fn_ref):
    gate = jax.nn.sigmoid(_dot(hn, wg_s[...]))
    h = h + gate * pp
    if fn_ref is not None:
        h = _rmsnorm(h, fn_ref[...])
    return h


def _run_tile(h_ref, p_ref, nm_ref, pn_ref, fn_ref, o_ref, act_ref, w_in_s, w_out_s, wg_s, wp_s,
              n_parts, mixer_block):
    n_sub = TM // SUB
    e = w_out_s.shape[0]
    n_blk = e // CB
    bw = n_parts * CB
    prev = None
    for k in range(n_sub + 1):
        rows = slice(k * SUB, (k + 1) * SUB)
        if k < n_sub:
            h = h_ref[rows, :]
            hn = _rmsnorm(h, nm_ref[...]).astype(BF16)
        for j in range(n_blk):
            if k < n_sub:
                proj = _dot(hn, w_in_s[:, j * bw:(j + 1) * bw])
            if prev is not None:
                if j == 0:
                    prev["h"] = prev["h"] + _dot(act_ref[prev["rows"], :], w_out_s[...])
                elif j == 1:
                    prev["pp"] = _dot(p_ref[prev["rows"], :].astype(BF16), wp_s[...])
                    prev["hn"] = _rmsnorm(prev["h"], pn_ref[...]).astype(BF16)
                elif j == 2:
                    o_ref[prev["rows"], :] = _ple_finish(prev["h"], prev["pp"], prev["hn"], wg_s, fn_ref)
            if k < n_sub:
                act_ref[rows, j * CB:(j + 1) * CB] = mixer_block(k, j, proj).astype(BF16)
        prev = {"h": h, "rows": rows} if k < n_sub else None


def _conv_layer_kernel(final_norm, widx, h_ref, p_ref, nm_ref, w_conv_ref, pn_ref, *rest):
    if final_norm:
        fn_ref, *rest = rest
    else:
        fn_ref = None
    (w_in_hbm, w_out_hbm, wg_hbm, wp_hbm, o_ref,
     w_in_s, w_out_s, wg_s, wp_s, act_ref, hist_ref, stage_ref, sem_ref) = rest
    w_in_hbm, w_out_hbm, wg_hbm, wp_hbm = (
        r.at[i] for r, i in zip((w_in_hbm, w_out_hbm, wg_hbm, wp_hbm), widx))

    @pl.when(_first_step())
    def _():
        chunks = (_weight_chunks(w_in_hbm, w_in_s, parts=4) + _weight_chunks(w_out_hbm, w_out_s)
                  + _weight_chunks(wg_hbm, wg_s) + _weight_chunks(wp_hbm, wp_s))
        _load_weights(chunks, stage_ref, sem_ref)

    @pl.when(pl.program_id(1) == 0)
    def _():
        hist_ref[...] = jnp.zeros_like(hist_ref)

    def mixer_block(k, j, proj):
        cs = slice(j * CB, (j + 1) * CB)
        b_g, c_g, v, z = (proj[:, q * CB:(q + 1) * CB] for q in range(4))
        u = c_g * v
        ext = jnp.concatenate([hist_ref[:, cs], u], axis=0)
        hist_ref[:, cs] = u[SUB - HIST:, :]
        w = w_conv_ref[:, cs]
        y = _shift_rows(ext, 2) * w[0:1] + _shift_rows(ext, 1) * w[1:2] + u * w[2:3]
        return (z * jax.nn.sigmoid(z)) * (b_g * y)

    _run_tile(h_ref, p_ref, nm_ref, pn_ref, fn_ref, o_ref, act_ref, w_in_s, w_out_s, wg_s, wp_s,
              4, mixer_block)


def _pool_layer_kernel(final_norm, widx, h_ref, p_ref, nm_ref, scale_ref, pn_ref, *rest):
    if final_norm:
        fn_ref, *rest = rest
    else:
        fn_ref = None
    (w_in_hbm, w_grp_hbm, w_out_hbm, wg_hbm, wp_hbm, o_ref,
     w_in_s, w_grp_s, w_out_s, wg_s, wp_s, act_ref, hist_ref, stage_ref, sem_ref) = rest
    w_in_hbm, w_grp_hbm, w_out_hbm, wg_hbm, wp_hbm = (
        r.at[i] for r, i in zip((w_in_hbm, w_grp_hbm, w_out_hbm, wg_hbm, wp_hbm), widx))

    @pl.when(_first_step())
    def _():
        chunks = _weight_chunks(w_in_hbm, w_in_s, parts=2)
        for g in range(len(POOL_WINDOWS)):
            chunks += _weight_chunks(w_grp_hbm.at[g], w_grp_s.at[g])
        chunks += (_weight_chunks(w_out_hbm, w_out_s) + _weight_chunks(wg_hbm, wg_s)
                   + _weight_chunks(wp_hbm, wp_s))
        _load_weights(chunks, stage_ref, sem_ref)

    s = pl.program_id(1)

    @pl.when(s == 0)
    def _():
        hist_ref[...] = jnp.zeros_like(hist_ref)

    row = lax.broadcasted_iota(jnp.int32, (SUB, CB), 0)

    def mixer_block(k, j, proj):
        cs = slice(j * CB, (j + 1) * CB)
        window = POOL_WINDOWS[j]
        u, z = proj[:, :CB], proj[:, CB:]
        ext = jnp.concatenate([hist_ref[:, cs], u], axis=0)
        hist_ref[:, cs] = u[SUB - HIST:, :]
        span = 1
        while span < window:
            ext = ext + pltpu.roll(ext, span, axis=0)
            span *= 2
        t1 = (row + (s * TM + k * SUB + 1)).astype(F32)
        d = ext[HIST:, :] * (1.0 / jnp.minimum(t1, float(window))) - u
        mixed = _dot(d.astype(BF16), w_grp_s[j]) * scale_ref[:, cs]
        return (z * jax.nn.sigmoid(z)) * mixed

    _run_tile(h_ref, p_ref, nm_ref, pn_ref, fn_ref, o_ref, act_ref, w_in_s, w_out_s, wg_s, wp_s,
              2, mixer_block)


def _layer_call(kernel_fn, h, p, layer, small_args, final_norm, weights, name):
    batch, seq, d = h.shape
    widx = tuple(i for _, i in weights)
    weights = [w for w, _ in weights]
    ple_dim = p.shape[-1]
    small = list(small_args) + ([final_norm] if final_norm is not None else [])
    const = lambda a: pl.BlockSpec(a.shape, lambda b, s: (0,) * a.ndim)
    in_specs = (
        [pl.BlockSpec((None, TM, d), lambda b, s: (b, s, 0)),
         pl.BlockSpec((None, None, TM, ple_dim), lambda b, s: (layer, b, s, 0))]
        + [const(a) for a in small]
        + [pl.BlockSpec(memory_space=pl.ANY)] * len(weights)
    )
    scratch = [pltpu.VMEM(w.shape[1:], BF16) for w in weights] + [
        pltpu.VMEM((TM, d), BF16),
        pltpu.VMEM((HIST, d), F32),
        pltpu.VMEM((2, STAGE_ROWS, STAGE_COLS), F32),
        pltpu.SemaphoreType.DMA((2,)),
    ]
    return pl.pallas_call(
        functools.partial(kernel_fn, final_norm is not None, widx),
        grid=(batch, seq // TM),
        in_specs=in_specs,
        out_specs=pl.BlockSpec((None, TM, d), lambda b, s: (b, s, 0)),
        out_shape=jax.ShapeDtypeStruct(h.shape, h.dtype),
        scratch_shapes=scratch,
        compiler_params=pltpu.CompilerParams(
            dimension_semantics=("arbitrary", "arbitrary"),
            vmem_limit_bytes=VMEM_LIMIT_BYTES,
        ),
        name=name,
    )(h, p, *small, *weights)


def kernel(x, p, norm_mix, a_w_in, a_w_conv, a_w_out, b_w_in, b_w_grp, b_scale, b_w_out, ple_norm,
           ple_w_gate, ple_w_proj, final_norm):
    depth = p.shape[0]
    assert x.shape[1] % TM == 0 and TM % SUB == 0 and SUB >= HIST >= max(POOL_WINDOWS) - 1
    row = lambda v: v.reshape(1, -1)
    h = x
    for i in range(depth):
        j = i // 2
        fn = row(final_norm) if i == depth - 1 else None
        if i % 2 == 0:
            h = _layer_call(_conv_layer_kernel, h, p, i,
                            (row(norm_mix[i]), a_w_conv[j], row(ple_norm[i])), fn,
                            ((a_w_in, j), (a_w_out, j), (ple_w_gate, i), (ple_w_proj, i)),
                            f"conv_layer_{i}")
        else:
            h = _layer_call(_pool_layer_kernel, h, p, i,
                            (row(norm_mix[i]), row(b_scale[j]), row(ple_norm[i])), fn,
                            ((b_w_in, j), (b_w_grp, j), (b_w_out, j), (ple_w_gate, i), (ple_w_proj, i)),
                            f"pool_layer_{i}")
    return h
```

```python
import functools

import jax
import jax.numpy as jnp
from jax import lax
from jax.experimental import pallas as pl
from jax.experimental.pallas import tpu as pltpu

EPS = 1e-6
POOL_WINDOWS = (2, 4, 8, 16)
HIST = 16
CB = 256
TM = 1024
SUB = 256
STAGE_ROWS = 256
STAGE_COLS = 1024
VMEM_LIMIT_BYTES = 56 * 1024 * 1024

F32 = jnp.float32
BF16 = jnp.bfloat16


def _rmsnorm(x, g):
    ms = jnp.mean(x * x, axis=-1, keepdims=True)
    return (x * lax.rsqrt(ms + EPS)) * g


def _dot(a, b):
    return jnp.dot(a, b, preferred_element_type=F32)


def _shift_rows(ext, k):
    return pltpu.roll(ext, k, axis=0)[HIST:, :]


def _weight_chunks(src, dst, parts=1):
    k, n = src.shape
    e = n // parts
    chunks = []
    for r0 in range(0, k, STAGE_ROWS):
        nr = min(STAGE_ROWS, k - r0)
        for c0 in range(0, n, STAGE_COLS):
            nc = min(STAGE_COLS, n - c0)

            def store(x, r0=r0, nr=nr, c0=c0, nc=nc):
                if parts == 1:
                    dst[r0:r0 + nr, c0:c0 + nc] = x
                    return
                for b0 in range(0, nc, CB):
                    part, j = divmod(c0 + b0, e)
                    d0 = (j // CB * parts + part) * CB
                    dst[r0:r0 + nr, d0:d0 + CB] = x[:, b0:b0 + CB]

            chunks.append((src.at[pl.ds(r0, nr), pl.ds(c0, nc)], nr, nc, store))
    return chunks


def _load_weights(chunks, stage_ref, sem_ref):
    def copy(i):
        src, nr, nc, _ = chunks[i]
        slot = i % 2
        return pltpu.make_async_copy(src, stage_ref.at[slot, pl.ds(0, nr), pl.ds(0, nc)], sem_ref.at[slot])

    copy(0).start()
    for i, (_, nr, nc, store) in enumerate(chunks):
        if i + 1 < len(chunks):
            copy(i + 1).start()
        copy(i).wait()
        store(stage_ref[i % 2, 0:nr, 0:nc].astype(BF16))


def _first_step():
    return jnp.logical_and(pl.program_id(0) == 0, pl.program_id(1) == 0)


def _sigmoid(x):
    return 0.5 * jnp.tanh(0.5 * x) + 0.5


def _silu(x):
    hx = 0.5 * x
    return hx * jnp.tanh(hx) + hx


def _ple_finish(h, pp, hn, wg_s, fn_ref):
    gate = _sigmoid(_dot(hn, wg_s[...]))
    h = h + gate * pp
    if fn_ref is not None:
        h = _rmsnorm(h, fn_ref[...])
    return h


def _run_tile(h_ref, p_ref, nm_ref, pn_ref, fn_ref, o_ref, act_ref, w_in_s, w_out_s, wg_s, wp_s,
              n_parts, mixer_pre, mixer_post=None):
    n_sub = TM // SUB
    n_blk = w_out_s.shape[0] // CB
    bw = n_parts * CB
    lag = 0 if mixer_post is None else 1
    n_units = n_sub * n_blk
    tail_start = {(k + 1) * n_blk + lag: k for k in range(n_sub)}
    sub = {}
    pending = None
    for t in range(n_units + lag + 2):
        if t < n_units:
            k, j = divmod(t, n_blk)
            rows = slice(k * SUB, (k + 1) * SUB)
            if j == 0:
                h = h_ref[rows, :]
                sub[k] = {"rows": rows, "h": h, "hn": _rmsnorm(h, nm_ref[...]).astype(BF16)}
            proj = _dot(sub[k]["hn"], w_in_s[:, j * bw:(j + 1) * bw])
        if pending is not None:
            pk, pj, state = pending
            act_ref[sub[pk]["rows"], pj * CB:(pj + 1) * CB] = mixer_post(pj, *state).astype(BF16)
            pending = None
        if t in tail_start:
            st = sub[tail_start[t]]
            st["h"] = st["h"] + _dot(act_ref[st["rows"], :], w_out_s[...])
            st["hn"] = _rmsnorm(st["h"], pn_ref[...]).astype(BF16)
        if t - 1 in tail_start:
            st = sub.pop(tail_start[t - 1])
            pp = _dot(p_ref[st["rows"], :].astype(BF16), wp_s[...])
            o_ref[st["rows"], :] = _ple_finish(st["h"], pp, st["hn"], wg_s, fn_ref)
        if t < n_units:
            state = mixer_pre(k, j, proj)
            if mixer_post is None:
                act_ref[rows, j * CB:(j + 1) * CB] = state.astype(BF16)
            else:
                pending = (k, j, state)


def _conv_layer_kernel(final_norm, widx, h_ref, p_ref, nm_ref, w_conv_ref, pn_ref, *rest):
    if final_norm:
        fn_ref, *rest = rest
    else:
        fn_ref = None
    (w_in_hbm, w_out_hbm, wg_hbm, wp_hbm, o_ref,
     w_in_s, w_out_s, wg_s, wp_s, act_ref, hist_ref, stage_ref, sem_ref) = rest
    w_in_hbm, w_out_hbm, wg_hbm, wp_hbm = (
        r.at[i] for r, i in zip((w_in_hbm, w_out_hbm, wg_hbm, wp_hbm), widx))

    @pl.when(_first_step())
    def _():
        chunks = (_weight_chunks(w_in_hbm, w_in_s, parts=4) + _weight_chunks(w_out_hbm, w_out_s)
                  + _weight_chunks(wg_hbm, wg_s) + _weight_chunks(wp_hbm, wp_s))
        _load_weights(chunks, stage_ref, sem_ref)

    @pl.when(pl.program_id(1) == 0)
    def _():
        hist_ref[...] = jnp.zeros_like(hist_ref)

    def mixer_pre(k, j, proj):
        cs = slice(j * CB, (j + 1) * CB)
        b_g, c_g, v, z = (proj[:, q * CB:(q + 1) * CB] for q in range(4))
        u = c_g * v
        ext = jnp.concatenate([hist_ref[:, cs], u], axis=0)
        hist_ref[:, cs] = u[SUB - HIST:, :]
        w = w_conv_ref[:, cs]
        y = _shift_rows(ext, 2) * w[0:1] + _shift_rows(ext, 1) * w[1:2] + u * w[2:3]
        return _silu(z) * (b_g * y)

    _run_tile(h_ref, p_ref, nm_ref, pn_ref, fn_ref, o_ref, act_ref, w_in_s, w_out_s, wg_s, wp_s,
              4, mixer_pre)


def _pool_layer_kernel(final_norm, widx, h_ref, p_ref, nm_ref, scale_ref, pn_ref, *rest):
    if final_norm:
        fn_ref, *rest = rest
    else:
        fn_ref = None
    (w_in_hbm, w_grp_hbm, w_out_hbm, wg_hbm, wp_hbm, o_ref,
     w_in_s, w_grp_s, w_out_s, wg_s, wp_s, act_ref, hist_ref, stage_ref, sem_ref) = rest
    w_in_hbm, w_grp_hbm, w_out_hbm, wg_hbm, wp_hbm = (
        r.at[i] for r, i in zip((w_in_hbm, w_grp_hbm, w_out_hbm, wg_hbm, wp_hbm), widx))

    @pl.when(_first_step())
    def _():
        chunks = _weight_chunks(w_in_hbm, w_in_s, parts=2)
        for g in range(len(POOL_WINDOWS)):
            chunks += _weight_chunks(w_grp_hbm.at[g], w_grp_s.at[g])
        chunks += (_weight_chunks(w_out_hbm, w_out_s) + _weight_chunks(wg_hbm, wg_s)
                   + _weight_chunks(wp_hbm, wp_s))
        _load_weights(chunks, stage_ref, sem_ref)

    s = pl.program_id(1)

    @pl.when(s == 0)
    def _():
        hist_ref[...] = jnp.zeros_like(hist_ref)

    row = lax.broadcasted_iota(jnp.int32, (SUB, CB), 0)

    def mixer_pre(k, j, proj):
        cs = slice(j * CB, (j + 1) * CB)
        window = POOL_WINDOWS[j]
        u, z = proj[:, :CB], proj[:, CB:]
        ext = jnp.concatenate([hist_ref[:, cs], u], axis=0)
        hist_ref[:, cs] = u[SUB - HIST:, :]
        span = 1
        while span < window:
            ext = ext + pltpu.roll(ext, span, axis=0)
            span *= 2
        t1 = (row + (s * TM + k * SUB + 1)).astype(F32)
        d = ext[HIST:, :] * (1.0 / jnp.minimum(t1, float(window))) - u
        return d.astype(BF16), _silu(z)

    def mixer_post(j, d, gate):
        mixed = _dot(d, w_grp_s[j]) * scale_ref[:, j * CB:(j + 1) * CB]
        return gate * mixed

    _run_tile(h_ref, p_ref, nm_ref, pn_ref, fn_ref, o_ref, act_ref, w_in_s, w_out_s, wg_s, wp_s,
              2, mixer_pre, mixer_post)


def _layer_call(kernel_fn, h, p, layer, small_args, final_norm, weights, name):
    batch, seq, d = h.shape
    widx = tuple(i for _, i in weights)
    weights = [w for w, _ in weights]
    ple_dim = p.shape[-1]
    small = list(small_args) + ([final_norm] if final_norm is not None else [])
    const = lambda a: pl.BlockSpec(a.shape, lambda b, s: (0,) * a.ndim)
    in_specs = (
        [pl.BlockSpec((None, TM, d), lambda b, s: (b, s, 0)),
         pl.BlockSpec((None, None, TM, ple_dim), lambda b, s: (layer, b, s, 0))]
        + [const(a) for a in small]
        + [pl.BlockSpec(memory_space=pl.ANY)] * len(weights)
    )
    scratch = [pltpu.VMEM(w.shape[1:], BF16) for w in weights] + [
        pltpu.VMEM((TM, d), BF16),
        pltpu.VMEM((HIST, d), F32),
        pltpu.VMEM((2, STAGE_ROWS, STAGE_COLS), F32),
        pltpu.SemaphoreType.DMA((2,)),
    ]
    return pl.pallas_call(
        functools.partial(kernel_fn, final_norm is not None, widx),
        grid=(batch, seq // TM),
        in_specs=in_specs,
        out_specs=pl.BlockSpec((None, TM, d), lambda b, s: (b, s, 0)),
        out_shape=jax.ShapeDtypeStruct(h.shape, h.dtype),
        scratch_shapes=scratch,
        compiler_params=pltpu.CompilerParams(
            dimension_semantics=("arbitrary", "arbitrary"),
            vmem_limit_bytes=VMEM_LIMIT_BYTES,
        ),
        name=name,
    )(h, p, *small, *weights)


def kernel(x, p, norm_mix, a_w_in, a_w_conv, a_w_out, b_w_in, b_w_grp, b_scale, b_w_out, ple_norm,
           ple_w_gate, ple_w_proj, final_norm):
    depth = p.shape[0]
    assert x.shape[1] % TM == 0 and TM % SUB == 0 and SUB >= HIST >= max(POOL_WINDOWS) - 1
    row = lambda v: v.reshape(1, -1)
    h = x
    for i in range(depth):
        j = i // 2
        fn = row(final_norm) if i == depth - 1 else None
        if i % 2 == 0:
            h = _layer_call(_conv_layer_kernel, h, p, i,
                            (row(norm_mix[i]), a_w_conv[j], row(ple_norm[i])), fn,
                            ((a_w_in, j), (a_w_out, j), (ple_w_gate, i), (ple_w_proj, i)),
                            f"conv_layer_{i}")
        else:
            h = _layer_call(_pool_layer_kernel, h, p, i,
                            (row(norm_mix[i]), row(b_scale[j]), row(ple_norm[i])), fn,
                            ((b_w_in, j), (b_w_grp, j), (b_w_out, j), (ple_w_gate, i), (ple_w_proj, i)),
                            f"pool_layer_{i}")
    return h
```

```python
import functools

import jax
import jax.numpy as jnp
from jax import lax
from jax.experimental import pallas as pl
from jax.experimental.pallas import tpu as pltpu

EPS = 1e-6
POOL_WINDOWS = (2, 4, 8, 16)
HIST = 16
CB = 256
TM = 1024
CONV_SUBS = (256, 256, 256, 256)
POOL_SUBS = (512, 512)
STAGE_ROWS = 256
STAGE_COLS = 1024
VMEM_LIMIT_BYTES = 56 * 1024 * 1024

F32 = jnp.float32
BF16 = jnp.bfloat16


def _rmsnorm(x, g):
    ms = jnp.mean(x * x, axis=-1, keepdims=True)
    return (x * lax.rsqrt(ms + EPS)) * g


def _dot(a, b):
    return jnp.dot(a, b, preferred_element_type=F32)


def _shift_rows(ext, k):
    return pltpu.roll(ext, k, axis=0)[HIST:, :]


def _weight_chunks(src, dst, parts=1):
    k, n = src.shape
    e = n // parts
    chunks = []
    for r0 in range(0, k, STAGE_ROWS):
        nr = min(STAGE_ROWS, k - r0)
        for c0 in range(0, n, STAGE_COLS):
            nc = min(STAGE_COLS, n - c0)

            def store(x, r0=r0, nr=nr, c0=c0, nc=nc):
                if parts == 1:
                    dst[r0:r0 + nr, c0:c0 + nc] = x
                    return
                for b0 in range(0, nc, CB):
                    part, j = divmod(c0 + b0, e)
                    d0 = (j // CB * parts + part) * CB
                    dst[r0:r0 + nr, d0:d0 + CB] = x[:, b0:b0 + CB]

            chunks.append((src.at[pl.ds(r0, nr), pl.ds(c0, nc)], nr, nc, store))
    return chunks


def _load_weights(chunks, stage_ref, sem_ref):
    def copy(i):
        src, nr, nc, _ = chunks[i]
        slot = i % 2
        return pltpu.make_async_copy(src, stage_ref.at[slot, pl.ds(0, nr), pl.ds(0, nc)], sem_ref.at[slot])

    copy(0).start()
    for i, (_, nr, nc, store) in enumerate(chunks):
        if i + 1 < len(chunks):
            copy(i + 1).start()
        copy(i).wait()
        store(stage_ref[i % 2, 0:nr, 0:nc].astype(BF16))


def _first_step():
    return jnp.logical_and(pl.program_id(0) == 0, pl.program_id(1) == 0)


def _sigmoid(x):
    return 0.5 * jnp.tanh(0.5 * x) + 0.5


def _silu(x):
    hx = 0.5 * x
    return hx * jnp.tanh(hx) + hx


def _ple_finish(h, pp, hn, wg_s, fn_ref):
    gate = _sigmoid(_dot(hn, wg_s[...]))
    h = h + gate * pp
    if fn_ref is not None:
        h = _rmsnorm(h, fn_ref[...])
    return h


def _run_tile(h_ref, p_ref, nm_ref, pn_ref, fn_ref, o_ref, act_ref, w_in_s, w_out_s, wg_s, wp_s,
              subs, n_parts, mixer_pre, mixer_post=None):
    n_sub = len(subs)
    starts = [sum(subs[:k]) for k in range(n_sub)]
    n_blk = w_out_s.shape[0] // CB
    bw = n_parts * CB
    lag = 0 if mixer_post is None else 1
    n_units = n_sub * n_blk
    tail_start = {(k + 1) * n_blk + lag: k for k in range(n_sub)}
    sub = {}
    pending = None
    for t in range(n_units + lag + 2):
        if t < n_units:
            k, j = divmod(t, n_blk)
            rows = slice(starts[k], starts[k] + subs[k])
            if j == 0:
                h = h_ref[rows, :]
                sub[k] = {"rows": rows, "h": h, "hn": _rmsnorm(h, nm_ref[...]).astype(BF16)}
            proj = _dot(sub[k]["hn"], w_in_s[:, j * bw:(j + 1) * bw])
        if pending is not None:
            pk, pj, state = pending
            act_ref[sub[pk]["rows"], pj * CB:(pj + 1) * CB] = mixer_post(pj, *state).astype(BF16)
            pending = None
        if t in tail_start:
            st = sub[tail_start[t]]
            st["h"] = st["h"] + _dot(act_ref[st["rows"], :], w_out_s[...])
            st["hn"] = _rmsnorm(st["h"], pn_ref[...]).astype(BF16)
        if t - 1 in tail_start:
            st = sub.pop(tail_start[t - 1])
            pp = _dot(p_ref[st["rows"], :].astype(BF16), wp_s[...])
            o_ref[st["rows"], :] = _ple_finish(st["h"], pp, st["hn"], wg_s, fn_ref)
        if t < n_units:
            state = mixer_pre(starts[k], j, proj)
            if mixer_post is None:
                act_ref[rows, j * CB:(j + 1) * CB] = state.astype(BF16)
            else:
                pending = (k, j, state)


def _conv_layer_kernel(final_norm, widx, h_ref, p_ref, nm_ref, w_conv_ref, pn_ref, *rest):
    if final_norm:
        fn_ref, *rest = rest
    else:
        fn_ref = None
    (w_in_hbm, w_out_hbm, wg_hbm, wp_hbm, o_ref,
     w_in_s, w_out_s, wg_s, wp_s, act_ref, hist_ref, stage_ref, sem_ref) = rest
    w_in_hbm, w_out_hbm, wg_hbm, wp_hbm = (
        r.at[i] for r, i in zip((w_in_hbm, w_out_hbm, wg_hbm, wp_hbm), widx))

    @pl.when(_first_step())
    def _():
        chunks = (_weight_chunks(w_in_hbm, w_in_s, parts=4) + _weight_chunks(w_out_hbm, w_out_s)
                  + _weight_chunks(wg_hbm, wg_s) + _weight_chunks(wp_hbm, wp_s))
        _load_weights(chunks, stage_ref, sem_ref)

    @pl.when(pl.program_id(1) == 0)
    def _():
        hist_ref[...] = jnp.zeros_like(hist_ref)

    def mixer_pre(r0, j, proj):
        cs = slice(j * CB, (j + 1) * CB)
        b_g, c_g, v, z = (proj[:, q * CB:(q + 1) * CB] for q in range(4))
        u = c_g * v
        ext = jnp.concatenate([hist_ref[:, cs], u], axis=0)
        hist_ref[:, cs] = u[u.shape[0] - HIST:, :]
        w = w_conv_ref[:, cs]
        y = _shift_rows(ext, 2) * w[0:1] + _shift_rows(ext, 1) * w[1:2] + u * w[2:3]
        return _silu(z) * (b_g * y)

    _run_tile(h_ref, p_ref, nm_ref, pn_ref, fn_ref, o_ref, act_ref, w_in_s, w_out_s, wg_s, wp_s,
              CONV_SUBS, 4, mixer_pre)


def _pool_layer_kernel(final_norm, widx, h_ref, p_ref, nm_ref, scale_ref, pn_ref, *rest):
    if final_norm:
        fn_ref, *rest = rest
    else:
        fn_ref = None
    (w_in_hbm, w_grp_hbm, w_out_hbm, wg_hbm, wp_hbm, o_ref,
     w_in_s, w_grp_s, w_out_s, wg_s, wp_s, act_ref, hist_ref, stage_ref, sem_ref) = rest
    w_in_hbm, w_grp_hbm, w_out_hbm, wg_hbm, wp_hbm = (
        r.at[i] for r, i in zip((w_in_hbm, w_grp_hbm, w_out_hbm, wg_hbm, wp_hbm), widx))

    @pl.when(_first_step())
    def _():
        chunks = _weight_chunks(w_in_hbm, w_in_s, parts=2)
        for g in range(len(POOL_WINDOWS)):
            chunks += _weight_chunks(w_grp_hbm.at[g], w_grp_s.at[g])
        chunks += (_weight_chunks(w_out_hbm, w_out_s) + _weight_chunks(wg_hbm, wg_s)
                   + _weight_chunks(wp_hbm, wp_s))
        _load_weights(chunks, stage_ref, sem_ref)

    s = pl.program_id(1)

    @pl.when(s == 0)
    def _():
        hist_ref[...] = jnp.zeros_like(hist_ref)

    def mixer_pre(r0, j, proj):
        cs = slice(j * CB, (j + 1) * CB)
        window = POOL_WINDOWS[j]
        u, z = proj[:, :CB], proj[:, CB:]
        ext = jnp.concatenate([hist_ref[:, cs], u], axis=0)
        hist_ref[:, cs] = u[u.shape[0] - HIST:, :]
        span = 1
        while span < window:
            ext = ext + pltpu.roll(ext, span, axis=0)
            span *= 2
        row = lax.broadcasted_iota(jnp.int32, u.shape, 0)
        t1 = (row + (s * TM + r0 + 1)).astype(F32)
        d = ext[HIST:, :] * (1.0 / jnp.minimum(t1, float(window))) - u
        return d.astype(BF16), _silu(z)

    def mixer_post(j, d, gate):
        mixed = _dot(d, w_grp_s[j]) * scale_ref[:, j * CB:(j + 1) * CB]
        return gate * mixed

    _run_tile(h_ref, p_ref, nm_ref, pn_ref, fn_ref, o_ref, act_ref, w_in_s, w_out_s, wg_s, wp_s,
              POOL_SUBS, 2, mixer_pre, mixer_post)


def _layer_call(kernel_fn, h, p, layer, small_args, final_norm, weights, name):
    batch, seq, d = h.shape
    widx = tuple(i for _, i in weights)
    weights = [w for w, _ in weights]
    ple_dim = p.shape[-1]
    small = list(small_args) + ([final_norm] if final_norm is not None else [])
    const = lambda a: pl.BlockSpec(a.shape, lambda b, s: (0,) * a.ndim)
    in_specs = (
        [pl.BlockSpec((None, TM, d), lambda b, s: (b, s, 0)),
         pl.BlockSpec((None, None, TM, ple_dim), lambda b, s: (layer, b, s, 0))]
        + [const(a) for a in small]
        + [pl.BlockSpec(memory_space=pl.ANY)] * len(weights)
    )
    scratch = [pltpu.VMEM(w.shape[1:], BF16) for w in weights] + [
        pltpu.VMEM((TM, d), BF16),
        pltpu.VMEM((HIST, d), F32),
        pltpu.VMEM((2, STAGE_ROWS, STAGE_COLS), F32),
        pltpu.SemaphoreType.DMA((2,)),
    ]
    return pl.pallas_call(
        functools.partial(kernel_fn, final_norm is not None, widx),
        grid=(batch, seq // TM),
        in_specs=in_specs,
        out_specs=pl.BlockSpec((None, TM, d), lambda b, s: (b, s, 0)),
        out_shape=jax.ShapeDtypeStruct(h.shape, h.dtype),
        scratch_shapes=scratch,
        compiler_params=pltpu.CompilerParams(
            dimension_semantics=("arbitrary", "arbitrary"),
            vmem_limit_bytes=VMEM_LIMIT_BYTES,
        ),
        name=name,
    )(h, p, *small, *weights)


def kernel(x, p, norm_mix, a_w_in, a_w_conv, a_w_out, b_w_in, b_w_grp, b_scale, b_w_out, ple_norm,
           ple_w_gate, ple_w_proj, final_norm):
    depth = p.shape[0]
    assert x.shape[1] % TM == 0 and HIST >= max(POOL_WINDOWS) - 1
    for subs in (CONV_SUBS, POOL_SUBS):
        assert sum(subs) == TM and all(n % HIST == 0 for n in subs)
    row = lambda v: v.reshape(1, -1)
    h = x
    for i in range(depth):
        j = i // 2
        fn = row(final_norm) if i == depth - 1 else None
        if i % 2 == 0:
            h = _layer_call(_conv_layer_kernel, h, p, i,
                            (row(norm_mix[i]), a_w_conv[j], row(ple_norm[i])), fn,
                            ((a_w_in, j), (a_w_out, j), (ple_w_gate, i), (ple_w_proj, i)),
                            f"conv_layer_{i}")
        else:
            h = _layer_call(_pool_layer_kernel, h, p, i,
                            (row(norm_mix[i]), row(b_scale[j]), row(ple_norm[i])), fn,
                            ((b_w_in, j), (b_w_grp, j), (b_w_out, j), (ple_w_gate, i), (ple_w_proj, i)),
                            f"pool_layer_{i}")
    return h
```

```python
import functools

import jax
import jax.numpy as jnp
from jax import lax
from jax.experimental import pallas as pl
from jax.experimental.pallas import tpu as pltpu

EPS = 1e-6
POOL_WINDOWS = (2, 4, 8, 16)
HIST = 16
CB = 256
LB = 128
CONV_PART_ORDER = (1, 2, 0, 3)
TM = 1024
CONV_SUBS = (256, 256, 256, 256)
POOL_SUBS = (512, 512)
STAGE_ROWS = 256
STAGE_COLS = 1024
STAGE_SLOTS = 4
VMEM_LIMIT_BYTES = 56 * 1024 * 1024

F32 = jnp.float32
BF16 = jnp.bfloat16


def _rmsnorm(x, g):
    ms = jnp.mean(x * x, axis=-1, keepdims=True)
    return (x * lax.rsqrt(ms + EPS)) * g


def _dot(a, b):
    return jnp.dot(a, b, preferred_element_type=F32)


def _shift_rows(ext, k):
    return pltpu.roll(ext, k, axis=0)[HIST:, :]


def _weight_chunks(src, dst, part_order=None, group=CB):
    k, n = src.shape
    chunks = []
    for r0 in range(0, k, STAGE_ROWS):
        nr = min(STAGE_ROWS, k - r0)
        for c0 in range(0, n, STAGE_COLS):
            nc = min(STAGE_COLS, n - c0)

            def store(x, r0=r0, nr=nr, c0=c0, nc=nc):
                if part_order is None:
                    dst[r0:r0 + nr, c0:c0 + nc] = x
                    return
                parts = len(part_order)
                e = n // parts
                for b0 in range(0, nc, group):
                    part, ch = divmod(c0 + b0, e)
                    d0 = (ch // group * parts + part_order.index(part)) * group
                    dst[r0:r0 + nr, d0:d0 + group] = x[:, b0:b0 + group]

            chunks.append((src.at[pl.ds(r0, nr), pl.ds(c0, nc)], nr, nc, store))
    return chunks


def _load_weights(chunks, stage_ref, sem_ref):
    n_slots = stage_ref.shape[0]

    def copy(i):
        src, nr, nc, _ = chunks[i]
        slot = i % n_slots
        return pltpu.make_async_copy(src, stage_ref.at[slot, pl.ds(0, nr), pl.ds(0, nc)], sem_ref.at[slot])

    for i in range(min(n_slots - 1, len(chunks))):
        copy(i).start()
    for i, (_, nr, nc, store) in enumerate(chunks):
        if i + n_slots - 1 < len(chunks):
            copy(i + n_slots - 1).start()
        copy(i).wait()
        store(stage_ref[i % n_slots, 0:nr, 0:nc].astype(BF16))


def _first_step():
    return jnp.logical_and(pl.program_id(0) == 0, pl.program_id(1) == 0)


def _sigmoid(x):
    return 0.5 * jnp.tanh(0.5 * x) + 0.5


def _silu(x):
    hx = 0.5 * x
    return hx * jnp.tanh(hx) + hx


def _ple_finish(h, pp, hn, wg_s, fn_ref):
    gate = _sigmoid(_dot(hn, wg_s[...]))
    h = h + gate * pp
    if fn_ref is not None:
        h = _rmsnorm(h, fn_ref[...])
    return h


def _run_tile(h_ref, p_ref, nm_ref, pn_ref, fn_ref, o_ref, act_ref, w_in_s, w_out_s, wg_s, wp_s,
              subs, n_parts, mixer_pre, mixer_post=None):
    n_sub = len(subs)
    starts = [sum(subs[:k]) for k in range(n_sub)]
    n_blk = w_out_s.shape[0] // CB
    bw = n_parts * CB
    lag = 0 if mixer_post is None else 1
    n_units = n_sub * n_blk
    tail_start = {(k + 1) * n_blk + lag: k for k in range(n_sub)}
    sub = {}
    pending = None
    for t in range(n_units + lag + 2):
        if t < n_units:
            k, j = divmod(t, n_blk)
            rows = slice(starts[k], starts[k] + subs[k])
            if j == 0:
                h = h_ref[rows, :]
                sub[k] = {"rows": rows, "h": h, "hn": _rmsnorm(h, nm_ref[...]).astype(BF16)}
            proj = _dot(sub[k]["hn"], w_in_s[:, j * bw:(j + 1) * bw])
        if pending is not None:
            pk, pj, state = pending
            act_ref[sub[pk]["rows"], pj * CB:(pj + 1) * CB] = mixer_post(pj, *state).astype(BF16)
            pending = None
        if t in tail_start:
            st = sub[tail_start[t]]
            st["h"] = st["h"] + _dot(act_ref[st["rows"], :], w_out_s[...])
            st["hn"] = _rmsnorm(st["h"], pn_ref[...]).astype(BF16)
        if t - 1 in tail_start:
            st = sub.pop(tail_start[t - 1])
            pp = _dot(p_ref[st["rows"], :].astype(BF16), wp_s[...])
            o_ref[st["rows"], :] = _ple_finish(st["h"], pp, st["hn"], wg_s, fn_ref)
        if t < n_units:
            state = mixer_pre(starts[k], j, proj)
            if mixer_post is None:
                act_ref[rows, j * CB:(j + 1) * CB] = state.astype(BF16)
            else:
                pending = (k, j, state)


def _conv_layer_kernel(final_norm, widx, h_ref, p_ref, nm_ref, w_conv_ref, pn_ref, *rest):
    if final_norm:
        fn_ref, *rest = rest
    else:
        fn_ref = None
    (w_in_hbm, w_out_hbm, wg_hbm, wp_hbm, o_ref,
     w_in_s, w_out_s, wg_s, wp_s, act_ref, hist_ref, stage_ref, sem_ref) = rest
    w_in_hbm, w_out_hbm, wg_hbm, wp_hbm = (
        r.at[i] for r, i in zip((w_in_hbm, w_out_hbm, wg_hbm, wp_hbm), widx))

    @pl.when(_first_step())
    def _():
        chunks = (_weight_chunks(w_in_hbm, w_in_s, CONV_PART_ORDER, LB) + _weight_chunks(w_out_hbm, w_out_s)
                  + _weight_chunks(wg_hbm, wg_s) + _weight_chunks(wp_hbm, wp_s))
        _load_weights(chunks, stage_ref, sem_ref)

    @pl.when(pl.program_id(1) == 0)
    def _():
        hist_ref[...] = jnp.zeros_like(hist_ref)

    def mixer_pre(r0, j, proj):
        halves = []
        for hh in range(CB // LB):
            cs = slice(j * CB + hh * LB, j * CB + (hh + 1) * LB)
            c_g, v, b_g, z = (proj[:, (4 * hh + q) * LB:(4 * hh + q + 1) * LB] for q in range(4))
            u = c_g * v
            ext = jnp.concatenate([hist_ref[:, cs], u], axis=0)
            hist_ref[:, cs] = u[u.shape[0] - HIST:, :]
            w = w_conv_ref[:, cs]
            y = _shift_rows(ext, 2) * w[0:1] + _shift_rows(ext, 1) * w[1:2] + u * w[2:3]
            halves.append(_silu(z) * (b_g * y))
        return jnp.concatenate(halves, axis=1)

    _run_tile(h_ref, p_ref, nm_ref, pn_ref, fn_ref, o_ref, act_ref, w_in_s, w_out_s, wg_s, wp_s,
              CONV_SUBS, 4, mixer_pre)


def _pool_layer_kernel(final_norm, widx, h_ref, p_ref, nm_ref, scale_ref, pn_ref, *rest):
    if final_norm:
        fn_ref, *rest = rest
    else:
        fn_ref = None
    (w_in_hbm, w_grp_hbm, w_out_hbm, wg_hbm, wp_hbm, o_ref,
     w_in_s, w_grp_s, w_out_s, wg_s, wp_s, act_ref, hist_ref, stage_ref, sem_ref) = rest
    w_in_hbm, w_grp_hbm, w_out_hbm, wg_hbm, wp_hbm = (
        r.at[i] for r, i in zip((w_in_hbm, w_grp_hbm, w_out_hbm, wg_hbm, wp_hbm), widx))

    @pl.when(_first_step())
    def _():
        chunks = _weight_chunks(w_in_hbm, w_in_s, (0, 1), CB)
        for g in range(len(POOL_WINDOWS)):
            chunks += _weight_chunks(w_grp_hbm.at[g], w_grp_s.at[g])
        chunks += (_weight_chunks(w_out_hbm, w_out_s) + _weight_chunks(wg_hbm, wg_s)
                   + _weight_chunks(wp_hbm, wp_s))
        _load_weights(chunks, stage_ref, sem_ref)

    s = pl.program_id(1)

    @pl.when(s == 0)
    def _():
        hist_ref[...] = jnp.zeros_like(hist_ref)

    def mixer_pre(r0, j, proj):
        cs = slice(j * CB, (j + 1) * CB)
        window = POOL_WINDOWS[j]
        u, z = proj[:, :CB], proj[:, CB:]
        ext = jnp.concatenate([hist_ref[:, cs], u], axis=0)
        hist_ref[:, cs] = u[u.shape[0] - HIST:, :]
        span = 1
        while span < window:
            ext = ext + pltpu.roll(ext, span, axis=0)
            span *= 2
        row = lax.broadcasted_iota(jnp.int32, u.shape, 0)
        t1 = (row + (s * TM + r0 + 1)).astype(F32)
        d = ext[HIST:, :] * (1.0 / jnp.minimum(t1, float(window))) - u
        return d.astype(BF16), _silu(z)

    def mixer_post(j, d, gate):
        mixed = _dot(d, w_grp_s[j]) * scale_ref[:, j * CB:(j + 1) * CB]
        return gate * mixed

    _run_tile(h_ref, p_ref, nm_ref, pn_ref, fn_ref, o_ref, act_ref, w_in_s, w_out_s, wg_s, wp_s,
              POOL_SUBS, 2, mixer_pre, mixer_post)


def _layer_call(kernel_fn, h, p, layer, small_args, final_norm, weights, name):
    batch, seq, d = h.shape
    widx = tuple(i for _, i in weights)
    weights = [w for w, _ in weights]
    ple_dim = p.shape[-1]
    small = list(small_args) + ([final_norm] if final_norm is not None else [])
    const = lambda a: pl.BlockSpec(a.shape, lambda b, s: (0,) * a.ndim)
    in_specs = (
        [pl.BlockSpec((None, TM, d), lambda b, s: (b, s, 0)),
         pl.BlockSpec((None, None, TM, ple_dim), lambda b, s: (layer, b, s, 0))]
        + [const(a) for a in small]
        + [pl.BlockSpec(memory_space=pl.ANY)] * len(weights)
    )
    scratch = [pltpu.VMEM(w.shape[1:], BF16) for w in weights] + [
        pltpu.VMEM((TM, d), BF16),
        pltpu.VMEM((HIST, d), F32),
        pltpu.VMEM((STAGE_SLOTS, STAGE_ROWS, STAGE_COLS), F32),
        pltpu.SemaphoreType.DMA((STAGE_SLOTS,)),
    ]
    return pl.pallas_call(
        functools.partial(kernel_fn, final_norm is not None, widx),
        grid=(batch, seq // TM),
        in_specs=in_specs,
        out_specs=pl.BlockSpec((None, TM, d), lambda b, s: (b, s, 0)),
        out_shape=jax.ShapeDtypeStruct(h.shape, h.dtype),
        scratch_shapes=scratch,
        compiler_params=pltpu.CompilerParams(
            dimension_semantics=("arbitrary", "arbitrary"),
            vmem_limit_bytes=VMEM_LIMIT_BYTES,
        ),
        name=name,
    )(h, p, *small, *weights)


def kernel(x, p, norm_mix, a_w_in, a_w_conv, a_w_out, b_w_in, b_w_grp, b_scale, b_w_out, ple_norm,
           ple_w_gate, ple_w_proj, final_norm):
    depth = p.shape[0]
    assert x.shape[1] % TM == 0 and HIST >= max(POOL_WINDOWS) - 1
    for subs in (CONV_SUBS, POOL_SUBS):
        assert sum(subs) == TM and all(n % HIST == 0 for n in subs)
    row = lambda v: v.reshape(1, -1)
    h = x
    for i in range(depth):
        j = i // 2
        fn = row(final_norm) if i == depth - 1 else None
        if i % 2 == 0:
            h = _layer_call(_conv_layer_kernel, h, p, i,
                            (row(norm_mix[i]), a_w_conv[j], row(ple_norm[i])), fn,
                            ((a_w_in, j), (a_w_out, j), (ple_w_gate, i), (ple_w_proj, i)),
                            f"conv_layer_{i}")
        else:
            h = _layer_call(_pool_layer_kernel, h, p, i,
                            (row(norm_mix[i]), row(b_scale[j]), row(ple_norm[i])), fn,
                            ((b_w_in, j), (b_w_grp, j), (b_w_out, j), (ple_w_gate, i), (ple_w_proj, i)),
                            f"pool_layer_{i}")
    return h
```

```python
import functools

import jax
import jax.numpy as jnp
from jax import lax
from jax.experimental import pallas as pl
from jax.experimental.pallas import tpu as pltpu

EPS = 1e-6
POOL_WINDOWS = (2, 4, 8, 16)
HIST = 16
CB = 256
LB = 128
CONV_PART_ORDER = (1, 2, 0, 3)
TM = 1024
CONV_SUBS = (256, 256, 256, 256)
POOL_SUBS = (512, 512)
STAGE_ROWS = 256
STAGE_COLS = 1024
STAGE_SLOTS = 4
VMEM_LIMIT_BYTES = 56 * 1024 * 1024

F32 = jnp.float32
BF16 = jnp.bfloat16


def _normalize(x):
    ms = jnp.mean(x * x, axis=-1, keepdims=True)
    return x * lax.rsqrt(ms + EPS)


def _dot(a, b):
    return jnp.dot(a, b, preferred_element_type=F32)


def _shift_rows(ext, k):
    return pltpu.roll(ext, k, axis=0)[HIST:, :]


def _half_silu(hz):
    return hz * (jnp.tanh(hz) + 1.0)


def _weight_chunks(src, dst, part_order=None, group=CB, part_scale=None, row_gain=None):
    k, n = src.shape
    parts = 1 if part_order is None else len(part_order)
    order = (0,) if part_order is None else part_order
    e = n // parts
    chunks = []
    for r0 in range(0, k, STAGE_ROWS):
        nr = min(STAGE_ROWS, k - r0)
        for c0 in range(0, n, STAGE_COLS):
            nc = min(STAGE_COLS, n - c0)

            def store(x, r0=r0, nr=nr, c0=c0, nc=nc):
                if row_gain is not None:
                    gain = jnp.concatenate([row_gain[r0:r0 + nr, :]] * (group // LB), axis=1)
                for b0 in range(0, nc, group):
                    part, ch = divmod(c0 + b0, e)
                    piece = x[:, b0:b0 + group]
                    if row_gain is not None:
                        piece = piece * gain
                    if part_scale is not None and part_scale[part] != 1.0:
                        piece = piece * part_scale[part]
                    d0 = (ch // group * parts + order.index(part)) * group
                    dst[r0:r0 + nr, d0:d0 + group] = piece.astype(BF16)

            chunks.append((src.at[pl.ds(r0, nr), pl.ds(c0, nc)], nr, nc, store))
    return chunks


def _load_weights(chunks, stage_ref, sem_ref):
    n_slots = stage_ref.shape[0]

    def copy(i):
        src, nr, nc, _ = chunks[i]
        slot = i % n_slots
        return pltpu.make_async_copy(src, stage_ref.at[slot, pl.ds(0, nr), pl.ds(0, nc)], sem_ref.at[slot])

    for i in range(min(n_slots - 1, len(chunks))):
        copy(i).start()
    for i, (_, nr, nc, store) in enumerate(chunks):
        if i + n_slots - 1 < len(chunks):
            copy(i + n_slots - 1).start()
        copy(i).wait()
        store(stage_ref[i % n_slots, 0:nr, 0:nc])


def _first_step():
    return jnp.logical_and(pl.program_id(0) == 0, pl.program_id(1) == 0)


def _ple_finish(h, half_pp, hn, wg_s, fn_ref):
    h = h + (jnp.tanh(_dot(hn, wg_s[...])) + 1.0) * half_pp
    if fn_ref is not None:
        h = _normalize(h) * fn_ref[...]
    return h


def _run_tile(h_ref, p_ref, fn_ref, o_ref, act_ref, w_in_s, w_out_s, wg_s, wp_s,
              subs, n_parts, mixer_pre, mixer_post=None):
    n_sub = len(subs)
    starts = [sum(subs[:k]) for k in range(n_sub)]
    n_blk = w_out_s.shape[0] // CB
    bw = n_parts * CB
    lag = 0 if mixer_post is None else 1
    n_units = n_sub * n_blk
    tail_start = {(k + 1) * n_blk + lag: k for k in range(n_sub)}
    sub = {}
    pending = None
    for t in range(n_units + lag + 2):
        if t < n_units:
            k, j = divmod(t, n_blk)
            rows = slice(starts[k], starts[k] + subs[k])
            if j == 0:
                h = h_ref[rows, :]
                sub[k] = {"rows": rows, "h": h, "hn": _normalize(h).astype(BF16)}
            proj = _dot(sub[k]["hn"], w_in_s[:, j * bw:(j + 1) * bw])
        if pending is not None:
            pk, pj, state = pending
            act_ref[sub[pk]["rows"], pj * CB:(pj + 1) * CB] = mixer_post(pj, *state).astype(BF16)
            pending = None
        if t in tail_start:
            st = sub[tail_start[t]]
            st["h"] = st["h"] + _dot(act_ref[st["rows"], :], w_out_s[...])
            st["hn"] = _normalize(st["h"]).astype(BF16)
        if t - 1 in tail_start:
            st = sub.pop(tail_start[t - 1])
            half_pp = _dot(p_ref[st["rows"], :].astype(BF16), wp_s[...])
            o_ref[st["rows"], :] = _ple_finish(st["h"], half_pp, st["hn"], wg_s, fn_ref)
        if t < n_units:
            state = mixer_pre(starts[k], j, proj)
            if mixer_post is None:
                act_ref[rows, j * CB:(j + 1) * CB] = state.astype(BF16)
            else:
                pending = (k, j, state)


def _conv_layer_kernel(final_norm, widx, h_ref, p_ref, nm_ref, w_conv_ref, pn_ref, *rest):
    if final_norm:
        fn_ref, *rest = rest
    else:
        fn_ref = None
    (w_in_hbm, w_out_hbm, wg_hbm, wp_hbm, o_ref,
     w_in_s, w_out_s, wg_s, wp_s, act_ref, hist_ref, stage_ref, sem_ref) = rest
    w_in_hbm, w_out_hbm, wg_hbm, wp_hbm = (
        r.at[i] for r, i in zip((w_in_hbm, w_out_hbm, wg_hbm, wp_hbm), widx))

    @pl.when(_first_step())
    def _():
        chunks = (_weight_chunks(w_in_hbm, w_in_s, CONV_PART_ORDER, LB, (1.0, 1.0, 1.0, 0.5), nm_ref)
                  + _weight_chunks(w_out_hbm, w_out_s)
                  + _weight_chunks(wg_hbm, wg_s, part_scale=(0.5,), row_gain=pn_ref)
                  + _weight_chunks(wp_hbm, wp_s, part_scale=(0.5,)))
        _load_weights(chunks, stage_ref, sem_ref)

    @pl.when(pl.program_id(1) == 0)
    def _():
        hist_ref[...] = jnp.zeros_like(hist_ref)

    def mixer_pre(r0, j, proj):
        halves = []
        for hh in range(CB // LB):
            cs = slice(j * CB + hh * LB, j * CB + (hh + 1) * LB)
            c_g, v, b_g, hz = (proj[:, (4 * hh + q) * LB:(4 * hh + q + 1) * LB] for q in range(4))
            u = c_g * v
            ext = jnp.concatenate([hist_ref[:, cs], u], axis=0)
            hist_ref[:, cs] = u[u.shape[0] - HIST:, :]
            w = w_conv_ref[:, cs]
            y = _shift_rows(ext, 2) * w[0:1] + _shift_rows(ext, 1) * w[1:2] + u * w[2:3]
            halves.append(_half_silu(hz) * (b_g * y))
        return jnp.concatenate(halves, axis=1)

    _run_tile(h_ref, p_ref, fn_ref, o_ref, act_ref, w_in_s, w_out_s, wg_s, wp_s,
              CONV_SUBS, 4, mixer_pre)


def _pool_layer_kernel(final_norm, widx, h_ref, p_ref, nm_ref, scale_ref, pn_ref, *rest):
    if final_norm:
        fn_ref, *rest = rest
    else:
        fn_ref = None
    (w_in_hbm, w_grp_hbm, w_out_hbm, wg_hbm, wp_hbm, o_ref,
     w_in_s, w_grp_s, w_out_s, wg_s, wp_s, act_ref, hist_ref, stage_ref, sem_ref) = rest
    w_in_hbm, w_grp_hbm, w_out_hbm, wg_hbm, wp_hbm = (
        r.at[i] for r, i in zip((w_in_hbm, w_grp_hbm, w_out_hbm, wg_hbm, wp_hbm), widx))

    @pl.when(_first_step())
    def _():
        chunks = _weight_chunks(w_in_hbm, w_in_s, (0, 1), CB, (1.0, 0.5), nm_ref)
        for g in range(len(POOL_WINDOWS)):
            chunks += _weight_chunks(w_grp_hbm.at[g], w_grp_s.at[g])
        chunks += (_weight_chunks(w_out_hbm, w_out_s)
                   + _weight_chunks(wg_hbm, wg_s, part_scale=(0.5,), row_gain=pn_ref)
                   + _weight_chunks(wp_hbm, wp_s, part_scale=(0.5,)))
        _load_weights(chunks, stage_ref, sem_ref)

    s = pl.program_id(1)

    @pl.when(s == 0)
    def _():
        hist_ref[...] = jnp.zeros_like(hist_ref)

    def mixer_pre(r0, j, proj):
        cs = slice(j * CB, (j + 1) * CB)
        window = POOL_WINDOWS[j]
        u, hz = proj[:, :CB], proj[:, CB:]
        ext = jnp.concatenate([hist_ref[:, cs], u], axis=0)
        hist_ref[:, cs] = u[u.shape[0] - HIST:, :]
        span = 1
        while span < window:
            ext = ext + pltpu.roll(ext, span, axis=0)
            span *= 2
        row = lax.broadcasted_iota(jnp.int32, u.shape, 0)
        t1 = (row + (s * TM + r0 + 1)).astype(F32)
        d = ext[HIST:, :] * (1.0 / jnp.minimum(t1, float(window))) - u
        return d.astype(BF16), _half_silu(hz)

    def mixer_post(j, d, gate):
        mixed = _dot(d, w_grp_s[j]) * scale_ref[:, j * CB:(j + 1) * CB]
        return gate * mixed

    _run_tile(h_ref, p_ref, fn_ref, o_ref, act_ref, w_in_s, w_out_s, wg_s, wp_s,
              POOL_SUBS, 2, mixer_pre, mixer_post)


def _layer_call(kernel_fn, h, p, layer, small_args, final_norm, weights, name):
    batch, seq, d = h.shape
    widx = tuple(i for _, i in weights)
    weights = [w for w, _ in weights]
    ple_dim = p.shape[-1]
    small = list(small_args) + ([final_norm] if final_norm is not None else [])
    const = lambda a: pl.BlockSpec(a.shape, lambda b, s: (0,) * a.ndim)
    in_specs = (
        [pl.BlockSpec((None, TM, d), lambda b, s: (b, s, 0)),
         pl.BlockSpec((None, None, TM, ple_dim), lambda b, s: (layer, b, s, 0))]
        + [const(a) for a in small]
        + [pl.BlockSpec(memory_space=pl.ANY)] * len(weights)
    )
    scratch = [pltpu.VMEM(w.shape[1:], BF16) for w in weights] + [
        pltpu.VMEM((TM, d), BF16),
        pltpu.VMEM((HIST, d), F32),
        pltpu.VMEM((STAGE_SLOTS, STAGE_ROWS, STAGE_COLS), F32),
        pltpu.SemaphoreType.DMA((STAGE_SLOTS,)),
    ]
    return pl.pallas_call(
        functools.partial(kernel_fn, final_norm is not None, widx),
        grid=(batch, seq // TM),
        in_specs=in_specs,
        out_specs=pl.BlockSpec((None, TM, d), lambda b, s: (b, s, 0)),
        out_shape=jax.ShapeDtypeStruct(h.shape, h.dtype),
        scratch_shapes=scratch,
        compiler_params=pltpu.CompilerParams(
            dimension_semantics=("arbitrary", "arbitrary"),
            vmem_limit_bytes=VMEM_LIMIT_BYTES,
        ),
        name=name,
    )(h, p, *small, *weights)


def kernel(x, p, norm_mix, a_w_in, a_w_conv, a_w_out, b_w_in, b_w_grp, b_scale, b_w_out, ple_norm,
           ple_w_gate, ple_w_proj, final_norm):
    depth = p.shape[0]
    assert x.shape[1] % TM == 0 and HIST >= max(POOL_WINDOWS) - 1
    for subs in (CONV_SUBS, POOL_SUBS):
        assert sum(subs) == TM and all(n % HIST == 0 for n in subs)
    row = lambda v: v.reshape(1, -1)
    col = lambda v: jnp.broadcast_to(v.reshape(-1, 1), (v.shape[0], LB))
    h = x
    for i in range(depth):
        j = i // 2
        fn = row(final_norm) if i == depth - 1 else None
        if i % 2 == 0:
            h = _layer_call(_conv_layer_kernel, h, p, i,
                            (col(norm_mix[i]), a_w_conv[j], col(ple_norm[i])), fn,
                            ((a_w_in, j), (a_w_out, j), (ple_w_gate, i), (ple_w_proj, i)),
                            f"conv_layer_{i}")
        else:
            h = _layer_call(_pool_layer_kernel, h, p, i,
                            (col(norm_mix[i]), row(b_scale[j]), col(ple_norm[i])), fn,
                            ((b_w_in, j), (b_w_grp, j), (b_w_out, j), (ple_w_gate, i), (ple_w_proj, i)),
                            f"pool_layer_{i}")
    return h
```

```python
import functools

import jax
import jax.numpy as jnp
from jax import lax
from jax.experimental import pallas as pl
from jax.experimental.pallas import tpu as pltpu

EPS = 1e-6
POOL_WINDOWS = (2, 4, 8, 16)
HIST = 16
CB = 256
LB = 128
CONV_PART_ORDER = (1, 2, 0, 3)
TM = 1024
CONV_SUBS = (256, 256, 256, 256)
POOL_SUBS = (512, 512)
STAGE_ROWS = 256
STAGE_COLS = 1024
STAGE_SLOTS = 4
VMEM_LIMIT_BYTES = 56 * 1024 * 1024

F32 = jnp.float32
BF16 = jnp.bfloat16


def _normalize(x):
    ms = jnp.mean(x * x, axis=-1, keepdims=True)
    return x * lax.rsqrt(ms + EPS)


def _dot(a, b):
    return jnp.dot(a, b, preferred_element_type=F32)


def _shift_rows(ext, k):
    return pltpu.roll(ext, k, axis=0)[HIST:, :]


def _half_silu(hz):
    return hz * (jnp.tanh(hz) + 1.0)


def _weight_chunks(src, dst, part_order=None, group=CB, part_scale=None, row_gain=None, col_gain=None):
    k, n = src.shape
    parts = 1 if part_order is None else len(part_order)
    order = (0,) if part_order is None else part_order
    e = n // parts
    chunks = []
    for r0 in range(0, k, STAGE_ROWS):
        nr = min(STAGE_ROWS, k - r0)
        for c0 in range(0, n, STAGE_COLS):
            nc = min(STAGE_COLS, n - c0)

            def store(x, r0=r0, nr=nr, c0=c0, nc=nc):
                if row_gain is not None:
                    gain = jnp.concatenate([row_gain[r0:r0 + nr, :]] * (group // LB), axis=1)
                for b0 in range(0, nc, group):
                    part, ch = divmod(c0 + b0, e)
                    piece = x[:, b0:b0 + group]
                    if row_gain is not None:
                        piece = piece * gain
                    if col_gain is not None:
                        piece = piece * col_gain[:, c0 + b0:c0 + b0 + group]
                    if part_scale is not None and part_scale[part] != 1.0:
                        piece = piece * part_scale[part]
                    d0 = (ch // group * parts + order.index(part)) * group
                    dst[r0:r0 + nr, d0:d0 + group] = piece.astype(BF16)

            chunks.append((src.at[pl.ds(r0, nr), pl.ds(c0, nc)], nr, nc, store))
    return chunks


def _load_weights(chunks, stage_ref, sem_ref):
    n_slots = stage_ref.shape[0]

    def copy(i):
        src, nr, nc, _ = chunks[i]
        slot = i % n_slots
        return pltpu.make_async_copy(src, stage_ref.at[slot, pl.ds(0, nr), pl.ds(0, nc)], sem_ref.at[slot])

    for i in range(min(n_slots - 1, len(chunks))):
        copy(i).start()
    for i, (_, nr, nc, store) in enumerate(chunks):
        if i + n_slots - 1 < len(chunks):
            copy(i + n_slots - 1).start()
        copy(i).wait()
        store(stage_ref[i % n_slots, 0:nr, 0:nc])


def _first_step():
    return jnp.logical_and(pl.program_id(0) == 0, pl.program_id(1) == 0)


def _ple_finish(h, half_pp, hn, wg_s, fn_ref):
    h = h + (jnp.tanh(_dot(hn, wg_s[...])) + 1.0) * half_pp
    if fn_ref is not None:
        h = _normalize(h) * fn_ref[...]
    return h


def _run_tile(h_ref, p_ref, fn_ref, o_ref, act_ref, w_in_s, w_out_s, wg_s, wp_s,
              subs, n_parts, mixer_pre, mixer_post=None):
    n_sub = len(subs)
    starts = [sum(subs[:k]) for k in range(n_sub)]
    n_blk = w_out_s.shape[0] // CB
    bw = n_parts * CB
    lag = 0 if mixer_post is None else 1
    n_units = n_sub * n_blk
    tail_start = {(k + 1) * n_blk + lag: k for k in range(n_sub)}
    sub = {}
    pending = None
    for t in range(n_units + lag + 2):
        if t < n_units:
            k, j = divmod(t, n_blk)
            rows = slice(starts[k], starts[k] + subs[k])
            if j == 0:
                h = h_ref[rows, :]
                sub[k] = {"rows": rows, "h": h, "hn": _normalize(h).astype(BF16)}
            proj = _dot(sub[k]["hn"], w_in_s[:, j * bw:(j + 1) * bw])
        if pending is not None:
            pk, pj, state = pending
            act_ref[sub[pk]["rows"], pj * CB:(pj + 1) * CB] = mixer_post(pj, *state).astype(BF16)
            pending = None
        if t in tail_start:
            st = sub[tail_start[t]]
            st["h"] = st["h"] + _dot(act_ref[st["rows"], :], w_out_s[...])
            st["hn"] = _normalize(st["h"]).astype(BF16)
        if t - 1 in tail_start:
            st = sub.pop(tail_start[t - 1])
            half_pp = _dot(p_ref[st["rows"], :].astype(BF16), wp_s[...])
            o_ref[st["rows"], :] = _ple_finish(st["h"], half_pp, st["hn"], wg_s, fn_ref)
        if t < n_units:
            state = mixer_pre(starts[k], j, proj)
            if mixer_post is None:
                act_ref[rows, j * CB:(j + 1) * CB] = state.astype(BF16)
            else:
                pending = (k, j, state)


def _conv_layer_kernel(final_norm, widx, h_ref, p_ref, nm_ref, w_conv_ref, pn_ref, *rest):
    if final_norm:
        fn_ref, *rest = rest
    else:
        fn_ref = None
    (w_in_hbm, w_out_hbm, wg_hbm, wp_hbm, o_ref,
     w_in_s, w_out_s, wg_s, wp_s, act_ref, hist_ref, stage_ref, sem_ref) = rest
    w_in_hbm, w_out_hbm, wg_hbm, wp_hbm = (
        r.at[i] for r, i in zip((w_in_hbm, w_out_hbm, wg_hbm, wp_hbm), widx))

    @pl.when(_first_step())
    def _():
        chunks = (_weight_chunks(w_in_hbm, w_in_s, CONV_PART_ORDER, LB, (1.0, 1.0, 1.0, 0.5), nm_ref)
                  + _weight_chunks(w_out_hbm, w_out_s)
                  + _weight_chunks(wg_hbm, wg_s, part_scale=(0.5,), row_gain=pn_ref)
                  + _weight_chunks(wp_hbm, wp_s, part_scale=(0.5,)))
        _load_weights(chunks, stage_ref, sem_ref)

    @pl.when(pl.program_id(1) == 0)
    def _():
        hist_ref[...] = jnp.zeros_like(hist_ref)

    def mixer_pre(r0, j, proj):
        halves = []
        for hh in range(CB // LB):
            cs = slice(j * CB + hh * LB, j * CB + (hh + 1) * LB)
            c_g, v, b_g, hz = (proj[:, (4 * hh + q) * LB:(4 * hh + q + 1) * LB] for q in range(4))
            u = c_g * v
            ext = jnp.concatenate([hist_ref[:, cs], u], axis=0)
            hist_ref[:, cs] = u[u.shape[0] - HIST:, :]
            w = w_conv_ref[:, cs]
            y = _shift_rows(ext, 2) * w[0:1] + _shift_rows(ext, 1) * w[1:2] + u * w[2:3]
            halves.append(_half_silu(hz) * (b_g * y))
        return jnp.concatenate(halves, axis=1)

    _run_tile(h_ref, p_ref, fn_ref, o_ref, act_ref, w_in_s, w_out_s, wg_s, wp_s,
              CONV_SUBS, 4, mixer_pre)


def _pool_layer_kernel(final_norm, widx, h_ref, p_ref, nm_ref, scale_ref, pn_ref, *rest):
    if final_norm:
        fn_ref, *rest = rest
    else:
        fn_ref = None
    (w_in_hbm, w_grp_hbm, w_out_hbm, wg_hbm, wp_hbm, o_ref,
     w_in_s, w_grp_s, w_out_s, wg_s, wp_s, act_ref, hist_ref, stage_ref, sem_ref) = rest
    w_in_hbm, w_grp_hbm, w_out_hbm, wg_hbm, wp_hbm = (
        r.at[i] for r, i in zip((w_in_hbm, w_grp_hbm, w_out_hbm, wg_hbm, wp_hbm), widx))

    @pl.when(_first_step())
    def _():
        chunks = _weight_chunks(w_in_hbm, w_in_s, (0, 1), CB, (1.0, 0.5), nm_ref)
        for g in range(len(POOL_WINDOWS)):
            chunks += _weight_chunks(w_grp_hbm.at[g], w_grp_s.at[g],
                                     col_gain=scale_ref.at[:, pl.ds(g * CB, CB)])
        chunks += (_weight_chunks(w_out_hbm, w_out_s)
                   + _weight_chunks(wg_hbm, wg_s, part_scale=(0.5,), row_gain=pn_ref)
                   + _weight_chunks(wp_hbm, wp_s, part_scale=(0.5,)))
        _load_weights(chunks, stage_ref, sem_ref)

    s = pl.program_id(1)

    @pl.when(s == 0)
    def _():
        hist_ref[...] = jnp.zeros_like(hist_ref)

    def mixer_pre(r0, j, proj):
        cs = slice(j * CB, (j + 1) * CB)
        window = POOL_WINDOWS[j]
        u, hz = proj[:, :CB], proj[:, CB:]
        ext = jnp.concatenate([hist_ref[:, cs], u], axis=0)
        hist_ref[:, cs] = u[u.shape[0] - HIST:, :]
        span = 1
        while span < window:
            ext = ext + pltpu.roll(ext, span, axis=0)
            span *= 2
        win = ext[HIST:, :]
        row = lax.broadcasted_iota(jnp.int32, (HIST, LB), 0)
        t1 = (row + (s * TM + r0 + 1)).astype(F32)
        inv_head = 1.0 / jnp.minimum(t1, float(window))
        head = win[:HIST, :] * jnp.concatenate([inv_head] * (CB // LB), axis=1)
        pooled = jnp.concatenate([head, win[HIST:, :] * (1.0 / window)], axis=0)
        return (pooled - u).astype(BF16), _half_silu(hz)

    def mixer_post(j, d, gate):
        return gate * _dot(d, w_grp_s[j])

    _run_tile(h_ref, p_ref, fn_ref, o_ref, act_ref, w_in_s, w_out_s, wg_s, wp_s,
              POOL_SUBS, 2, mixer_pre, mixer_post)


def _layer_call(kernel_fn, h, p, layer, small_args, final_norm, weights, name):
    batch, seq, d = h.shape
    widx = tuple(i for _, i in weights)
    weights = [w for w, _ in weights]
    ple_dim = p.shape[-1]
    small = list(small_args) + ([final_norm] if final_norm is not None else [])
    const = lambda a: pl.BlockSpec(a.shape, lambda b, s: (0,) * a.ndim)
    in_specs = (
        [pl.BlockSpec((None, TM, d), lambda b, s: (b, s, 0)),
         pl.BlockSpec((None, None, TM, ple_dim), lambda b, s: (layer, b, s, 0))]
        + [const(a) for a in small]
        + [pl.BlockSpec(memory_space=pl.ANY)] * len(weights)
    )
    scratch = [pltpu.VMEM(w.shape[1:], BF16) for w in weights] + [
        pltpu.VMEM((TM, d), BF16),
        pltpu.VMEM((HIST, d), F32),
        pltpu.VMEM((STAGE_SLOTS, STAGE_ROWS, STAGE_COLS), F32),
        pltpu.SemaphoreType.DMA((STAGE_SLOTS,)),
    ]
    return pl.pallas_call(
        functools.partial(kernel_fn, final_norm is not None, widx),
        grid=(batch, seq // TM),
        in_specs=in_specs,
        out_specs=pl.BlockSpec((None, TM, d), lambda b, s: (b, s, 0)),
        out_shape=jax.ShapeDtypeStruct(h.shape, h.dtype),
        scratch_shapes=scratch,
        compiler_params=pltpu.CompilerParams(
            dimension_semantics=("arbitrary", "arbitrary"),
            vmem_limit_bytes=VMEM_LIMIT_BYTES,
        ),
        name=name,
    )(h, p, *small, *weights)


def kernel(x, p, norm_mix, a_w_in, a_w_conv, a_w_out, b_w_in, b_w_grp, b_scale, b_w_out, ple_norm,
           ple_w_gate, ple_w_proj, final_norm):
    depth = p.shape[0]
    assert x.shape[1] % TM == 0 and HIST >= max(POOL_WINDOWS) - 1
    for subs in (CONV_SUBS, POOL_SUBS):
        assert sum(subs) == TM and all(n % HIST == 0 for n in subs)
    row = lambda v: v.reshape(1, -1)
    col = lambda v: jnp.broadcast_to(v.reshape(-1, 1), (v.shape[0], LB))
    h = x
    for i in range(depth):
        j = i // 2
        fn = row(final_norm) if i == depth - 1 else None
        if i % 2 == 0:
            h = _layer_call(_conv_layer_kernel, h, p, i,
                            (col(norm_mix[i]), a_w_conv[j], col(ple_norm[i])), fn,
                            ((a_w_in, j), (a_w_out, j), (ple_w_gate, i), (ple_w_proj, i)),
                            f"conv_layer_{i}")
        else:
            h = _layer_call(_pool_layer_kernel, h, p, i,
                            (col(norm_mix[i]), row(b_scale[j]), col(ple_norm[i])), fn,
                            ((b_w_in, j), (b_w_grp, j), (b_w_out, j), (ple_w_gate, i), (ple_w_proj, i)),
                            f"pool_layer_{i}")
    return h
```

```python
import functools

import jax
import jax.numpy as jnp
from jax import lax
from jax.experimental import pallas as pl
from jax.experimental.pallas import tpu as pltpu

EPS = 1e-6
POOL_WINDOWS = (2, 4, 8, 16)
HIST = 16
CB = 256
LB = 128
CONV_PART_ORDER = (1, 2, 0, 3)
TM = 1024
CONV_SUBS = (256, 256, 256, 256)
POOL_SUBS = (512, 512)
STAGE_ROWS = 256
STAGE_COLS = 1024
STAGE_SLOTS = 4
VMEM_LIMIT_BYTES = 56 * 1024 * 1024

F32 = jnp.float32
BF16 = jnp.bfloat16


def _normalize(x):
    ms = jnp.mean(x * x, axis=-1, keepdims=True)
    return x * lax.rsqrt(ms + EPS)


def _dot(a, b):
    return jnp.dot(a, b, preferred_element_type=F32)


def _shift_rows(ext, k):
    return pltpu.roll(ext, k, axis=0)[HIST:, :]


def _half_silu(hz):
    return hz * (jnp.tanh(hz) + 1.0)


def _weight_chunks(src, dst, part_order=None, group=CB, part_scale=None, row_gain=None, col_gain=None):
    k, n = src.shape
    parts = 1 if part_order is None else len(part_order)
    order = (0,) if part_order is None else part_order
    e = n // parts
    chunks = []
    for r0 in range(0, k, STAGE_ROWS):
        nr = min(STAGE_ROWS, k - r0)
        for c0 in range(0, n, STAGE_COLS):
            nc = min(STAGE_COLS, n - c0)

            def store(x, r0=r0, nr=nr, c0=c0, nc=nc):
                if row_gain is not None:
                    gain = jnp.concatenate([row_gain[r0:r0 + nr, :]] * (group // LB), axis=1)
                for b0 in range(0, nc, group):
                    part, ch = divmod(c0 + b0, e)
                    piece = x[:, b0:b0 + group]
                    if row_gain is not None:
                        piece = piece * gain
                    if col_gain is not None:
                        piece = piece * col_gain[:, c0 + b0:c0 + b0 + group]
                    if part_scale is not None and part_scale[part] != 1.0:
                        piece = piece * part_scale[part]
                    d0 = (ch // group * parts + order.index(part)) * group
                    dst[r0:r0 + nr, d0:d0 + group] = piece.astype(BF16)

            chunks.append((src.at[pl.ds(r0, nr), pl.ds(c0, nc)], nr, nc, store))
    return chunks


def _load_weights(chunks, stage_ref, sem_ref):
    n_slots = stage_ref.shape[0]

    def copy(i):
        src, nr, nc, _ = chunks[i]
        slot = i % n_slots
        return pltpu.make_async_copy(src, stage_ref.at[slot, pl.ds(0, nr), pl.ds(0, nc)], sem_ref.at[slot])

    for i in range(min(n_slots - 1, len(chunks))):
        copy(i).start()
    for i, (_, nr, nc, store) in enumerate(chunks):
        if i + n_slots - 1 < len(chunks):
            copy(i + n_slots - 1).start()
        copy(i).wait()
        store(stage_ref[i % n_slots, 0:nr, 0:nc])


def _first_step():
    return jnp.logical_and(pl.program_id(0) == 0, pl.program_id(1) == 0)


def _ple_finish(h, half_pp, hn, wg_s, fn_ref):
    h = h + (jnp.tanh(_dot(hn, wg_s[...])) + 1.0) * half_pp
    if fn_ref is not None:
        h = _normalize(h) * fn_ref[...]
    return h


def _run_tile(h_ref, p_ref, fn_ref, o_ref, act_ref, w_in_s, w_out_s, wg_s, wp_s,
              subs, n_parts, mixer_pre, mixer_post=None):
    n_sub = len(subs)
    starts = [sum(subs[:k]) for k in range(n_sub)]
    n_blk = w_out_s.shape[0] // CB
    bw = n_parts * CB
    lag = 0 if mixer_post is None else 1
    n_units = n_sub * n_blk
    tail_start = {(k + 1) * n_blk + lag: k for k in range(n_sub)}
    sub = {}
    pending = None
    for t in range(n_units + lag + 2):
        if t < n_units:
            k, j = divmod(t, n_blk)
            rows = slice(starts[k], starts[k] + subs[k])
            if j == 0:
                h = h_ref[rows, :]
                sub[k] = {"rows": rows, "h": h, "hn": _normalize(h).astype(BF16)}
            if t == 0:
                sub[k]["half_pp"] = _dot(p_ref[rows, :].astype(BF16), wp_s[...])
            proj = _dot(sub[k]["hn"], w_in_s[:, j * bw:(j + 1) * bw])
        if pending is not None:
            pk, pj, state = pending
            act_ref[sub[pk]["rows"], pj * CB:(pj + 1) * CB] = mixer_post(pj, *state).astype(BF16)
            pending = None
        if t in tail_start:
            st = sub[tail_start[t]]
            st["h"] = st["h"] + _dot(act_ref[st["rows"], :], w_out_s[...])
            st["hn"] = _normalize(st["h"]).astype(BF16)
        if t - 1 in tail_start:
            st = sub.pop(tail_start[t - 1])
            if "half_pp" not in st:
                st["half_pp"] = _dot(p_ref[st["rows"], :].astype(BF16), wp_s[...])
            o_ref[st["rows"], :] = _ple_finish(st["h"], st["half_pp"], st["hn"], wg_s, fn_ref)
        if t < n_units:
            state = mixer_pre(starts[k], j, proj)
            if mixer_post is None:
                act_ref[rows, j * CB:(j + 1) * CB] = state.astype(BF16)
            else:
                pending = (k, j, state)


def _conv_layer_kernel(final_norm, widx, h_ref, p_ref, nm_ref, w_conv_ref, pn_ref, *rest):
    if final_norm:
        fn_ref, *rest = rest
    else:
        fn_ref = None
    (w_in_hbm, w_out_hbm, wg_hbm, wp_hbm, o_ref,
     w_in_s, w_out_s, wg_s, wp_s, act_ref, hist_ref, stage_ref, sem_ref) = rest
    w_in_hbm, w_out_hbm, wg_hbm, wp_hbm = (
        r.at[i] for r, i in zip((w_in_hbm, w_out_hbm, wg_hbm, wp_hbm), widx))

    @pl.when(_first_step())
    def _():
        chunks = (_weight_chunks(w_in_hbm, w_in_s, CONV_PART_ORDER, LB, (1.0, 1.0, 1.0, 0.5), nm_ref)
                  + _weight_chunks(w_out_hbm, w_out_s)
                  + _weight_chunks(wg_hbm, wg_s, part_scale=(0.5,), row_gain=pn_ref)
                  + _weight_chunks(wp_hbm, wp_s, part_scale=(0.5,)))
        _load_weights(chunks, stage_ref, sem_ref)

    @pl.when(pl.program_id(1) == 0)
    def _():
        hist_ref[...] = jnp.zeros_like(hist_ref)

    def mixer_pre(r0, j, proj):
        halves = []
        for hh in range(CB // LB):
            cs = slice(j * CB + hh * LB, j * CB + (hh + 1) * LB)
            c_g, v, b_g, hz = (proj[:, (4 * hh + q) * LB:(4 * hh + q + 1) * LB] for q in range(4))
            u = c_g * v
            ext = jnp.concatenate([hist_ref[:, cs], u], axis=0)
            hist_ref[:, cs] = u[u.shape[0] - HIST:, :]
            w = w_conv_ref[:, cs]
            y = _shift_rows(ext, 2) * w[0:1] + _shift_rows(ext, 1) * w[1:2] + u * w[2:3]
            halves.append(_half_silu(hz) * (b_g * y))
        return jnp.concatenate(halves, axis=1)

    _run_tile(h_ref, p_ref, fn_ref, o_ref, act_ref, w_in_s, w_out_s, wg_s, wp_s,
              CONV_SUBS, 4, mixer_pre)


def _pool_layer_kernel(final_norm, widx, h_ref, p_ref, nm_ref, scale_ref, pn_ref, *rest):
    if final_norm:
        fn_ref, *rest = rest
    else:
        fn_ref = None
    (w_in_hbm, w_grp_hbm, w_out_hbm, wg_hbm, wp_hbm, o_ref,
     w_in_s, w_grp_s, w_out_s, wg_s, wp_s, act_ref, hist_ref, stage_ref, sem_ref) = rest
    w_in_hbm, w_grp_hbm, w_out_hbm, wg_hbm, wp_hbm = (
        r.at[i] for r, i in zip((w_in_hbm, w_grp_hbm, w_out_hbm, wg_hbm, wp_hbm), widx))

    @pl.when(_first_step())
    def _():
        chunks = _weight_chunks(w_in_hbm, w_in_s, (0, 1), CB, (1.0, 0.5), nm_ref)
        for g in range(len(POOL_WINDOWS)):
            chunks += _weight_chunks(w_grp_hbm.at[g], w_grp_s.at[g],
                                     col_gain=scale_ref.at[:, pl.ds(g * CB, CB)])
        chunks += (_weight_chunks(w_out_hbm, w_out_s)
                   + _weight_chunks(wg_hbm, wg_s, part_scale=(0.5,), row_gain=pn_ref)
                   + _weight_chunks(wp_hbm, wp_s, part_scale=(0.5,)))
        _load_weights(chunks, stage_ref, sem_ref)

    s = pl.program_id(1)

    @pl.when(s == 0)
    def _():
        hist_ref[...] = jnp.zeros_like(hist_ref)

    def mixer_pre(r0, j, proj):
        cs = slice(j * CB, (j + 1) * CB)
        window = POOL_WINDOWS[j]
        u, hz = proj[:, :CB], proj[:, CB:]
        ext = jnp.concatenate([hist_ref[:, cs], u], axis=0)
        hist_ref[:, cs] = u[u.shape[0] - HIST:, :]
        span = 1
        while span < window:
            ext = ext + pltpu.roll(ext, span, axis=0)
            span *= 2
        win = ext[HIST:, :]
        row = lax.broadcasted_iota(jnp.int32, (HIST, LB), 0)
        t1 = (row + (s * TM + r0 + 1)).astype(F32)
        inv_head = 1.0 / jnp.minimum(t1, float(window))
        head = win[:HIST, :] * jnp.concatenate([inv_head] * (CB // LB), axis=1)
        pooled = jnp.concatenate([head, win[HIST:, :] * (1.0 / window)], axis=0)
        return (pooled - u).astype(BF16), _half_silu(hz)

    def mixer_post(j, d, gate):
        return gate * _dot(d, w_grp_s[j])

    _run_tile(h_ref, p_ref, fn_ref, o_ref, act_ref, w_in_s, w_out_s, wg_s, wp_s,
              POOL_SUBS, 2, mixer_pre, mixer_post)


def _layer_call(kernel_fn, h, p, layer, small_args, final_norm, weights, name):
    batch, seq, d = h.shape
    widx = tuple(i for _, i in weights)
    weights = [w for w, _ in weights]
    ple_dim = p.shape[-1]
    small = list(small_args) + ([final_norm] if final_norm is not None else [])
    const = lambda a: pl.BlockSpec(a.shape, lambda b, s: (0,) * a.ndim)
    in_specs = (
        [pl.BlockSpec((None, TM, d), lambda b, s: (b, s, 0)),
         pl.BlockSpec((None, None, TM, ple_dim), lambda b, s: (layer, b, s, 0))]
        + [const(a) for a in small]
        + [pl.BlockSpec(memory_space=pl.ANY)] * len(weights)
    )
    scratch = [pltpu.VMEM(w.shape[1:], BF16) for w in weights] + [
        pltpu.VMEM((TM, d), BF16),
        pltpu.VMEM((HIST, d), F32),
        pltpu.VMEM((STAGE_SLOTS, STAGE_ROWS, STAGE_COLS), F32),
        pltpu.SemaphoreType.DMA((STAGE_SLOTS,)),
    ]
    return pl.pallas_call(
        functools.partial(kernel_fn, final_norm is not None, widx),
        grid=(batch, seq // TM),
        in_specs=in_specs,
        out_specs=pl.BlockSpec((None, TM, d), lambda b, s: (b, s, 0)),
        out_shape=jax.ShapeDtypeStruct(h.shape, h.dtype),
        scratch_shapes=scratch,
        compiler_params=pltpu.CompilerParams(
            dimension_semantics=("arbitrary", "arbitrary"),
            vmem_limit_bytes=VMEM_LIMIT_BYTES,
        ),
        name=name,
    )(h, p, *small, *weights)


def kernel(x, p, norm_mix, a_w_in, a_w_conv, a_w_out, b_w_in, b_w_grp, b_scale, b_w_out, ple_norm,
           ple_w_gate, ple_w_proj, final_norm):
    depth = p.shape[0]
    assert x.shape[1] % TM == 0 and HIST >= max(POOL_WINDOWS) - 1
    for subs in (CONV_SUBS, POOL_SUBS):
        assert sum(subs) == TM and all(n % HIST == 0 for n in subs)
    row = lambda v: v.reshape(1, -1)
    col = lambda v: jnp.broadcast_to(v.reshape(-1, 1), (v.shape[0], LB))
    h = x
    for i in range(depth):
        j = i // 2
        fn = row(final_norm) if i == depth - 1 else None
        if i % 2 == 0:
            h = _layer_call(_conv_layer_kernel, h, p, i,
                            (col(norm_mix[i]), a_w_conv[j], col(ple_norm[i])), fn,
                            ((a_w_in, j), (a_w_out, j), (ple_w_gate, i), (ple_w_proj, i)),
                            f"conv_layer_{i}")
        else:
            h = _layer_call(_pool_layer_kernel, h, p, i,
                            (col(norm_mix[i]), row(b_scale[j]), col(ple_norm[i])), fn,
                            ((b_w_in, j), (b_w_grp, j), (b_w_out, j), (ple_w_gate, i), (ple_w_proj, i)),
                            f"pool_layer_{i}")
    return h
```

```python
import jax
import jax.numpy as jnp
from jax import lax
from jax.experimental import pallas as pl
from jax.experimental.pallas import tpu as pltpu

EPS = 1e-6
POOL_WINDOWS = (2, 4, 8, 16)
HIST = 16
CB = 256
LB = 128
CONV_PART_ORDER = (1, 2, 0, 3)
TM = 1024
CONV_SUBS = (256, 256, 256, 256)
POOL_SUBS = (512, 512)
STAGE_ROWS = 1024
STAGE_COLS = 1024
RING_ROWS = 256
VMEM_LIMIT_BYTES = 60 * 1024 * 1024

F32 = jnp.float32
BF16 = jnp.bfloat16


def _normalize(x):
    ms = jnp.mean(x * x, axis=-1, keepdims=True)
    return x * lax.rsqrt(ms + EPS)


def _dot(a, b):
    return jnp.dot(a, b, preferred_element_type=F32)


def _shift_rows(ext, k):
    return pltpu.roll(ext, k, axis=0)[HIST:, :]


def _half_silu(hz):
    return hz * (jnp.tanh(hz) + 1.0)


def _weight_chunks(src, dst, chunk_rows, part_order=None, group=CB, part_scale=None, row_gain=None,
                   col_gain=None):
    k, n = src.shape
    parts = 1 if part_order is None else len(part_order)
    order = (0,) if part_order is None else part_order
    e = n // parts
    chunks = []
    for r0 in range(0, k, chunk_rows):
        nr = min(chunk_rows, k - r0)
        for c0 in range(0, n, STAGE_COLS):
            nc = min(STAGE_COLS, n - c0)

            def store(x, r0=r0, nr=nr, c0=c0, nc=nc):
                if row_gain is not None:
                    gain = jnp.concatenate([row_gain[r0:r0 + nr, :]] * (group // LB), axis=1)
                for b0 in range(0, nc, group):
                    part, ch = divmod(c0 + b0, e)
                    piece = x[:, b0:b0 + group]
                    if row_gain is not None:
                        piece = piece * gain
                    if col_gain is not None:
                        piece = piece * col_gain[:, c0 + b0:c0 + b0 + group]
                    if part_scale is not None and part_scale[part] != 1.0:
                        piece = piece * part_scale[part]
                    d0 = (ch // group * parts + order.index(part)) * group
                    dst[r0:r0 + nr, d0:d0 + group] = piece.astype(BF16)

            chunks.append((src.at[pl.ds(r0, nr), pl.ds(c0, nc)], nr, nc, store))
    return chunks


def _conv_chunks(rows, a_w_in, a_w_out, wg, wp, cw, nm_ref, pn_ref):
    w_in_s, w_out_s, wg_s, wp_s = cw
    return (_weight_chunks(a_w_in, w_in_s, rows, CONV_PART_ORDER, LB, (1.0, 1.0, 1.0, 0.5), nm_ref)
            + _weight_chunks(a_w_out, w_out_s, rows)
            + _weight_chunks(wg, wg_s, rows, part_scale=(0.5,), row_gain=pn_ref)
            + _weight_chunks(wp, wp_s, rows, part_scale=(0.5,)))


def _pool_chunks(rows, b_w_in, b_w_grp, b_w_out, wg, wp, pw, nm_ref, pn_ref, scale_ref):
    w_in_s, w_grp_s, w_out_s, wg_s, wp_s = pw
    chunks = _weight_chunks(b_w_in, w_in_s, rows, (0, 1), CB, (1.0, 0.5), nm_ref)
    for g in range(len(POOL_WINDOWS)):
        chunks += _weight_chunks(b_w_grp.at[g], w_grp_s.at[g], rows,
                                 col_gain=scale_ref.at[:, pl.ds(g * CB, CB)])
    return (chunks + _weight_chunks(b_w_out, w_out_s, rows)
            + _weight_chunks(wg, wg_s, rows, part_scale=(0.5,), row_gain=pn_ref)
            + _weight_chunks(wp, wp_s, rows, part_scale=(0.5,)))


def _load_weights(chunks, stage_ref, sem_ref):
    n_slots = stage_ref.shape[0] // RING_ROWS

    def copy(i):
        src, nr, nc, _ = chunks[i]
        slot = i % n_slots
        return pltpu.make_async_copy(src, stage_ref.at[pl.ds(slot * RING_ROWS, nr), pl.ds(0, nc)],
                                     sem_ref.at[slot])

    for i in range(min(n_slots - 1, len(chunks))):
        copy(i).start()
    for i, (_, nr, nc, store) in enumerate(chunks):
        if i + n_slots - 1 < len(chunks):
            copy(i + n_slots - 1).start()
        copy(i).wait()
        store(stage_ref.at[pl.ds((i % n_slots) * RING_ROWS, nr), pl.ds(0, nc)])


def _stage_next_layer(chunks, step, stage_ref, sem_ref):
    def copy(i):
        src, nr, nc, _ = chunks[i]
        return pltpu.make_async_copy(src, stage_ref.at[pl.ds(0, nr), pl.ds(0, nc)], sem_ref.at[0])

    @pl.when(step <= len(chunks))
    def _():
        for i in range(len(chunks) + 1):
            @pl.when(step == i)
            def _(i=i):
                if i >= 1:
                    _, nr, nc, store = chunks[i - 1]
                    copy(i - 1).wait()
                    store(stage_ref.at[pl.ds(0, nr), pl.ds(0, nc)])
                if i < len(chunks):
                    copy(i).start()


def _ple_finish(h, half_pp, hn, wg_s, fn_ref):
    h = h + (jnp.tanh(_dot(hn, wg_s[...])) + 1.0) * half_pp
    if fn_ref is not None:
        h = _normalize(h) * fn_ref[...]
    return h


def _run_tile(h_ref, p_ref, fn_ref, o_ref, act_ref, w_in_s, w_out_s, wg_s, wp_s,
              subs, n_parts, mixer_pre, mixer_post=None):
    n_sub = len(subs)
    starts = [sum(subs[:k]) for k in range(n_sub)]
    n_blk = w_out_s.shape[0] // CB
    bw = n_parts * CB
    lag = 0 if mixer_post is None else 1
    n_units = n_sub * n_blk
    tail_start = {(k + 1) * n_blk + lag: k for k in range(n_sub)}
    sub = {}
    pending = None
    for t in range(n_units + lag + 2):
        if t < n_units:
            k, j = divmod(t, n_blk)
            rows = slice(starts[k], starts[k] + subs[k])
            if j == 0:
                h = h_ref[rows, :]
                sub[k] = {"rows": rows, "h": h, "hn": _normalize(h).astype(BF16)}
            if t == 0:
                sub[k]["half_pp"] = _dot(p_ref[rows, :].astype(BF16), wp_s[...])
            proj = _dot(sub[k]["hn"], w_in_s[:, j * bw:(j + 1) * bw])
        if pending is not None:
            pk, pj, state = pending
            act_ref[sub[pk]["rows"], pj * CB:(pj + 1) * CB] = mixer_post(pj, *state).astype(BF16)
            pending = None
        if t in tail_start:
            st = sub[tail_start[t]]
            st["h"] = st["h"] + _dot(act_ref[st["rows"], :], w_out_s[...])
            st["hn"] = _normalize(st["h"]).astype(BF16)
        if t - 1 in tail_start:
            st = sub.pop(tail_start[t - 1])
            if "half_pp" not in st:
                st["half_pp"] = _dot(p_ref[st["rows"], :].astype(BF16), wp_s[...])
            o_ref[st["rows"], :] = _ple_finish(st["h"], st["half_pp"], st["hn"], wg_s, fn_ref)
        if t < n_units:
            state = mixer_pre(starts[k], j, proj)
            if mixer_post is None:
                act_ref[rows, j * CB:(j + 1) * CB] = state.astype(BF16)
            else:
                pending = (k, j, state)


def _conv_tile(h_ref, p_ref, o_ref, w_conv_ref, cw, act_ref, hist_ref):
    w_in_s, w_out_s, wg_s, wp_s = cw

    def mixer_pre(r0, j, proj):
        halves = []
        for hh in range(CB // LB):
            cs = slice(j * CB + hh * LB, j * CB + (hh + 1) * LB)
            c_g, v, b_g, hz = (proj[:, (4 * hh + q) * LB:(4 * hh + q + 1) * LB] for q in range(4))
            u = c_g * v
            ext = jnp.concatenate([hist_ref[:, cs], u], axis=0)
            hist_ref[:, cs] = u[u.shape[0] - HIST:, :]
            w = w_conv_ref[:, cs]
            y = _shift_rows(ext, 2) * w[0:1] + _shift_rows(ext, 1) * w[1:2] + u * w[2:3]
            halves.append(_half_silu(hz) * (b_g * y))
        return jnp.concatenate(halves, axis=1)

    _run_tile(h_ref, p_ref, None, o_ref, act_ref, w_in_s, w_out_s, wg_s, wp_s,
              CONV_SUBS, 4, mixer_pre)


def _pool_tile(h_ref, p_ref, o_ref, fn_ref, pw, act_ref, hist_ref, tile_pos):
    w_in_s, w_grp_s, w_out_s, wg_s, wp_s = pw

    def mixer_pre(r0, j, proj):
        cs = slice(j * CB, (j + 1) * CB)
        window = POOL_WINDOWS[j]
        u, hz = proj[:, :CB], proj[:, CB:]
        ext = jnp.concatenate([hist_ref[:, cs], u], axis=0)
        hist_ref[:, cs] = u[u.shape[0] - HIST:, :]
        span = 1
        while span < window:
            ext = ext + pltpu.roll(ext, span, axis=0)
            span *= 2
        win = ext[HIST:, :]
        row = lax.broadcasted_iota(jnp.int32, (HIST, LB), 0)
        t1 = (row + (tile_pos + r0 + 1)).astype(F32)
        inv_head = 1.0 / jnp.minimum(t1, float(window))
        head = win[:HIST, :] * jnp.concatenate([inv_head] * (CB // LB), axis=1)
        pooled = jnp.concatenate([head, win[HIST:, :] * (1.0 / window)], axis=0)
        return (pooled - u).astype(BF16), _half_silu(hz)

    def mixer_post(j, d, gate):
        return gate * _dot(d, w_grp_s[j])

    _run_tile(h_ref, p_ref, fn_ref, o_ref, act_ref, w_in_s, w_out_s, wg_s, wp_s,
              POOL_SUBS, 2, mixer_pre, mixer_post)


def _trunk_kernel(x_hbm, p_ref, nm_ref, pn_ref, w_conv_ref, scale_ref, fn_ref,
                  a_w_in, a_w_out, b_w_in, b_w_grp, b_w_out, ple_wg, ple_wp,
                  out_hbm,
                  cw_in_s, cw_out_s, cwg_s, cwp_s, pw_in_s, pw_grp_s, pw_out_s, pwg_s, pwp_s,
                  act_ref, hist_ref, stage_ref, stage_sem, h_in, in_sem, o_buf, out_sem):
    cw = (cw_in_s, cw_out_s, cwg_s, cwp_s)
    pw = (pw_in_s, pw_grp_s, pw_out_s, pwg_s, pwp_s)
    lyr, b, s = pl.program_id(0), pl.program_id(1), pl.program_id(2)
    n_b, n_s = pl.num_programs(1), pl.num_programs(2)
    per_layer = n_b * n_s
    step = b * n_s + s
    g = lyr * per_layer + step
    last = pl.num_programs(0) * per_layer - 1
    slot = lax.rem(g, 2)

    def in_copy(src, bb, ss, sl):
        return pltpu.make_async_copy(src.at[bb, pl.ds(ss * TM, TM), :], h_in.at[sl], in_sem.at[sl])

    def out_copy(bb, ss, sl):
        return pltpu.make_async_copy(o_buf.at[sl], out_hbm.at[bb, pl.ds(ss * TM, TM), :], out_sem.at[sl])

    @pl.when(g == 0)
    def _():
        in_copy(x_hbm, 0, 0, 0).start()
        _load_weights(_conv_chunks(RING_ROWS, a_w_in.at[0], a_w_out.at[0], ple_wg.at[0], ple_wp.at[0],
                                   cw, nm_ref, pn_ref), stage_ref, stage_sem)

    nxt = step + 1
    nxt_b, nxt_s = nxt // n_s, lax.rem(nxt, n_s)

    @pl.when(jnp.logical_and(lyr == 0, nxt < per_layer))
    def _():
        in_copy(x_hbm, nxt_b, nxt_s, 1 - slot).start()

    @pl.when(jnp.logical_and(lyr > 0, nxt < per_layer))
    def _():
        in_copy(out_hbm, nxt_b, nxt_s, 1 - slot).start()

    @pl.when(jnp.logical_and(nxt == per_layer, g < last))
    def _():
        in_copy(out_hbm, 0, 0, 1 - slot).start()

    @pl.when(lyr == 0)
    def _():
        in_copy(x_hbm, b, s, slot).wait()

    @pl.when(lyr > 0)
    def _():
        in_copy(out_hbm, b, s, slot).wait()

    @pl.when(s == 0)
    def _():
        hist_ref[...] = jnp.zeros_like(hist_ref)

    h_ref, o_ref = h_in.at[slot], o_buf.at[slot]

    @pl.when(lax.rem(lyr, 2) == 0)
    def _():
        _conv_tile(h_ref, p_ref, o_ref, w_conv_ref, cw, act_ref, hist_ref)
        j = lyr // 2
        _stage_next_layer(_pool_chunks(STAGE_ROWS, b_w_in.at[j], b_w_grp.at[j], b_w_out.at[j],
                                       ple_wg.at[lyr + 1], ple_wp.at[lyr + 1], pw, nm_ref, pn_ref,
                                       scale_ref), step, stage_ref, stage_sem)

    @pl.when(lyr == 1)
    def _():
        _pool_tile(h_ref, p_ref, o_ref, None, pw, act_ref, hist_ref, s * TM)
        _stage_next_layer(_conv_chunks(STAGE_ROWS, a_w_in.at[1], a_w_out.at[1], ple_wg.at[2],
                                       ple_wp.at[2], cw, nm_ref, pn_ref), step, stage_ref, stage_sem)

    @pl.when(lyr == 3)
    def _():
        _pool_tile(h_ref, p_ref, o_ref, fn_ref, pw, act_ref, hist_ref, s * TM)

    @pl.when(g >= 1)
    def _():
        out_copy(b, s, 1 - slot).wait()

    out_copy(b, s, slot).start()

    @pl.when(g == last)
    def _():
        out_copy(b, s, slot).wait()


def kernel(x, p, norm_mix, a_w_in, a_w_conv, a_w_out, b_w_in, b_w_grp, b_scale, b_w_out, ple_norm,
           ple_w_gate, ple_w_proj, final_norm):
    depth, batch, seq, ple_dim = p.shape
    d = x.shape[2]
    assert depth == 4 and seq % TM == 0 and HIST >= max(POOL_WINDOWS) - 1
    for subs in (CONV_SUBS, POOL_SUBS):
        assert sum(subs) == TM and all(n % HIST == 0 for n in subs)
    lanes = lambda g: jnp.broadcast_to(g[:, :, None], (depth, d, LB))

    def staged(lyr, b, s):
        first = jnp.logical_and(lyr == 0, jnp.logical_and(b == 0, s == 0))
        return jnp.where(first, 0, jnp.minimum(lyr + 1, depth - 1))

    weights = (a_w_in, a_w_out, b_w_in, b_w_grp, b_w_out, ple_w_gate, ple_w_proj)
    in_specs = [
        pl.BlockSpec(memory_space=pl.ANY),
        pl.BlockSpec((None, None, TM, ple_dim), lambda l, b, s: (l, b, s, 0)),
        pl.BlockSpec((None, d, LB), lambda l, b, s: (staged(l, b, s), 0, 0)),
        pl.BlockSpec((None, d, LB), lambda l, b, s: (staged(l, b, s), 0, 0)),
        pl.BlockSpec((None,) + a_w_conv.shape[1:], lambda l, b, s: (l // 2, 0, 0)),
        pl.BlockSpec((None, 1, d), lambda l, b, s: (l // 2, 0, 0)),
        pl.BlockSpec((1, d), lambda l, b, s: (0, 0)),
    ] + [pl.BlockSpec(memory_space=pl.ANY)] * len(weights)
    wshape = lambda w: pltpu.VMEM(w.shape[1:], BF16)
    scratch = [
        wshape(a_w_in), wshape(a_w_out), wshape(ple_w_gate), wshape(ple_w_proj),
        wshape(b_w_in), wshape(b_w_grp), wshape(b_w_out), wshape(ple_w_gate), wshape(ple_w_proj),
        pltpu.VMEM((TM, d), BF16),
        pltpu.VMEM((HIST, d), F32),
        pltpu.VMEM((STAGE_ROWS, STAGE_COLS), F32),
        pltpu.SemaphoreType.DMA((STAGE_ROWS // RING_ROWS,)),
        pltpu.VMEM((2, TM, d), F32),
        pltpu.SemaphoreType.DMA((2,)),
        pltpu.VMEM((2, TM, d), F32),
        pltpu.SemaphoreType.DMA((2,)),
    ]
    return pl.pallas_call(
        _trunk_kernel,
        grid=(depth, batch, seq // TM),
        in_specs=in_specs,
        out_specs=pl.BlockSpec(memory_space=pl.ANY),
        out_shape=jax.ShapeDtypeStruct(x.shape, x.dtype),
        scratch_shapes=scratch,
        compiler_params=pltpu.CompilerParams(
            dimension_semantics=("arbitrary", "arbitrary", "arbitrary"),
            vmem_limit_bytes=VMEM_LIMIT_BYTES,
        ),
        name="trunk",
    )(x, p, lanes(norm_mix), lanes(ple_norm), a_w_conv, b_scale.reshape(-1, 1, d),
      final_norm.reshape(1, d), *weights)
```

```python
import functools

import jax
import jax.numpy as jnp
from jax import lax
from jax.experimental import pallas as pl
from jax.experimental.pallas import tpu as pltpu

EPS = 1e-6
POOL_WINDOWS = (2, 4, 8, 16)
HIST = 16
CB = 256
LB = 128
CONV_PART_ORDER = (1, 2, 0, 3)
TM = 1024
CONV_SUBS = (256, 256, 256, 256)
POOL_SUBS = (512, 512)
STAGE_ROWS = 256
STAGE_COLS = 1024
STAGE_SLOTS = 4
VMEM_LIMIT_BYTES = 56 * 1024 * 1024

F32 = jnp.float32
BF16 = jnp.bfloat16


def _normalize(x):
    ms = jnp.mean(x * x, axis=-1, keepdims=True)
    return x * lax.rsqrt(ms + EPS)


def _dot(a, b):
    return jnp.dot(a, b, preferred_element_type=F32)


def _dot_f32(a, b):
    a_hi, b_hi = a.astype(BF16), b.astype(BF16)
    a_lo = (a - a_hi.astype(F32)).astype(BF16)
    b_lo = (b - b_hi.astype(F32)).astype(BF16)
    return _dot(a_hi, b_hi) + (_dot(a_hi, b_lo) + _dot(a_lo, b_hi))


def _shift_rows(ext, k):
    return pltpu.roll(ext, k, axis=0)[HIST:, :]


def _half_silu(hz):
    return hz * (jnp.tanh(hz) + 1.0)


def _weight_chunks(src, dst, part_order=None, group=CB, part_scale=None, row_gain=None, col_gain=None):
    k, n = src.shape
    parts = 1 if part_order is None else len(part_order)
    order = (0,) if part_order is None else part_order
    e = n // parts
    chunks = []
    for r0 in range(0, k, STAGE_ROWS):
        nr = min(STAGE_ROWS, k - r0)
        for c0 in range(0, n, STAGE_COLS):
            nc = min(STAGE_COLS, n - c0)

            def store(x, r0=r0, nr=nr, c0=c0, nc=nc):
                if row_gain is not None:
                    gain = jnp.concatenate([row_gain[r0:r0 + nr, :]] * (group // LB), axis=1)
                for b0 in range(0, nc, group):
                    part, ch = divmod(c0 + b0, e)
                    piece = x[:, b0:b0 + group]
                    if row_gain is not None:
                        piece = piece * gain
                    if col_gain is not None:
                        piece = piece * col_gain[:, c0 + b0:c0 + b0 + group]
                    if part_scale is not None and part_scale[part] != 1.0:
                        piece = piece * part_scale[part]
                    d0 = (ch // group * parts + order.index(part)) * group
                    dst[r0:r0 + nr, d0:d0 + group] = piece.astype(BF16)

            chunks.append((src.at[pl.ds(r0, nr), pl.ds(c0, nc)], nr, nc, store))
    return chunks


def _load_weights(chunks, stage_ref, sem_ref):
    n_slots = stage_ref.shape[0]

    def copy(i):
        src, nr, nc, _ = chunks[i]
        slot = i % n_slots
        return pltpu.make_async_copy(src, stage_ref.at[slot, pl.ds(0, nr), pl.ds(0, nc)], sem_ref.at[slot])

    for i in range(min(n_slots - 1, len(chunks))):
        copy(i).start()
    for i, (_, nr, nc, store) in enumerate(chunks):
        if i + n_slots - 1 < len(chunks):
            copy(i + n_slots - 1).start()
        copy(i).wait()
        store(stage_ref[i % n_slots, 0:nr, 0:nc])


def _first_step():
    return jnp.logical_and(pl.program_id(0) == 0, pl.program_id(1) == 0)


def _ple_finish(h, half_pp, hn, wg_s, fn_ref):
    h = h + (jnp.tanh(_dot(hn, wg_s[...])) + 1.0) * half_pp
    if fn_ref is not None:
        h = _normalize(h) * fn_ref[...]
    return h


def _run_tile(h_ref, p_ref, fn_ref, o_ref, act_ref, w_in_s, w_out_s, wg_s, wp_s,
              subs, n_parts, mixer_pre, mixer_post=None):
    n_sub = len(subs)
    starts = [sum(subs[:k]) for k in range(n_sub)]
    n_blk = w_out_s.shape[0] // CB
    bw = n_parts * CB
    lag = 0 if mixer_post is None else 1
    n_units = n_sub * n_blk
    tail_start = {(k + 1) * n_blk + lag: k for k in range(n_sub)}
    sub = {}
    pending = None
    for t in range(n_units + lag + 2):
        if t < n_units:
            k, j = divmod(t, n_blk)
            rows = slice(starts[k], starts[k] + subs[k])
            if j == 0:
                h = h_ref[rows, :]
                sub[k] = {"rows": rows, "h": h, "hn": _normalize(h).astype(BF16)}
            if t == 0:
                sub[k]["half_pp"] = _dot(p_ref[rows, :].astype(BF16), wp_s[...])
            proj = _dot(sub[k]["hn"], w_in_s[:, j * bw:(j + 1) * bw])
        if pending is not None:
            pk, pj, state = pending
            act_ref[sub[pk]["rows"], pj * CB:(pj + 1) * CB] = mixer_post(pj, *state).astype(BF16)
            pending = None
        if t in tail_start:
            st = sub[tail_start[t]]
            st["h"] = st["h"] + _dot(act_ref[st["rows"], :], w_out_s[...])
            st["hn"] = _normalize(st["h"]).astype(BF16)
        if t - 1 in tail_start:
            st = sub.pop(tail_start[t - 1])
            if "half_pp" not in st:
                st["half_pp"] = _dot(p_ref[st["rows"], :].astype(BF16), wp_s[...])
            o_ref[st["rows"], :] = _ple_finish(st["h"], st["half_pp"], st["hn"], wg_s, fn_ref)
        if t < n_units:
            state = mixer_pre(starts[k], j, proj)
            if mixer_post is None:
                act_ref[rows, j * CB:(j + 1) * CB] = state.astype(BF16)
            else:
                pending = (k, j, state)


def _conv_layer_kernel(final_norm, widx, h_ref, p_ref, nm_ref, w_conv_ref, pn_ref, *rest):
    if final_norm:
        fn_ref, *rest = rest
    else:
        fn_ref = None
    (w_in_hbm, w_out_hbm, wg_hbm, wp_hbm, o_ref,
     w_in_s, w_out_s, wg_s, wp_s, act_ref, hist_ref, stage_ref, sem_ref) = rest
    w_in_hbm, w_out_hbm, wg_hbm, wp_hbm = (
        r.at[i] for r, i in zip((w_in_hbm, w_out_hbm, wg_hbm, wp_hbm), widx))

    @pl.when(_first_step())
    def _():
        chunks = (_weight_chunks(w_in_hbm, w_in_s, CONV_PART_ORDER, LB, (1.0, 1.0, 1.0, 0.5), nm_ref)
                  + _weight_chunks(w_out_hbm, w_out_s)
                  + _weight_chunks(wg_hbm, wg_s, part_scale=(0.5,), row_gain=pn_ref)
                  + _weight_chunks(wp_hbm, wp_s, part_scale=(0.5,)))
        _load_weights(chunks, stage_ref, sem_ref)

    @pl.when(pl.program_id(1) == 0)
    def _():
        hist_ref[...] = jnp.zeros_like(hist_ref)

    def mixer_pre(r0, j, proj):
        halves = []
        for hh in range(CB // LB):
            cs = slice(j * CB + hh * LB, j * CB + (hh + 1) * LB)
            c_g, v, b_g, hz = (proj[:, (4 * hh + q) * LB:(4 * hh + q + 1) * LB] for q in range(4))
            u = c_g * v
            ext = jnp.concatenate([hist_ref[:, cs], u], axis=0)
            hist_ref[:, cs] = u[u.shape[0] - HIST:, :]
            w = w_conv_ref[:, cs]
            y = _shift_rows(ext, 2) * w[0:1] + _shift_rows(ext, 1) * w[1:2] + u * w[2:3]
            halves.append(_half_silu(hz) * (b_g * y))
        return jnp.concatenate(halves, axis=1)

    _run_tile(h_ref, p_ref, fn_ref, o_ref, act_ref, w_in_s, w_out_s, wg_s, wp_s,
              CONV_SUBS, 4, mixer_pre)


def _pool_layer_kernel(final_norm, widx, h_ref, p_ref, nm_ref, scale_ref, pn_ref, *rest):
    if final_norm:
        fn_ref, *rest = rest
    else:
        fn_ref = None
    (w_in_hbm, w_grp_hbm, w_out_hbm, wg_hbm, wp_hbm, o_ref,
     w_in_s, w_grp_s, w_out_s, wg_s, wp_s, act_ref, hist_ref, stage_ref, sem_ref) = rest
    w_in_hbm, w_grp_hbm, w_out_hbm, wg_hbm, wp_hbm = (
        r.at[i] for r, i in zip((w_in_hbm, w_grp_hbm, w_out_hbm, wg_hbm, wp_hbm), widx))

    n_grp = len(POOL_WINDOWS)
    e = n_grp * CB

    def store_grp(x, g):
        w_grp_s[g] = x[...] * scale_ref[:, g * CB:(g + 1) * CB]

    def store_value(x, r0, nr):
        gain = jnp.concatenate([nm_ref[r0:r0 + nr, :]] * (CB // LB), axis=1)
        for g in range(n_grp):
            fused = _dot_f32(x[:, g * CB:(g + 1) * CB] * gain, w_grp_s[g])
            for hh in range(CB // LB):
                q = g * (CB // LB) + hh
                w_in_s[r0:r0 + nr, 2 * q * LB:(2 * q + 1) * LB] = fused[:, hh * LB:(hh + 1) * LB].astype(BF16)

    def store_gate(x, r0, nr):
        gain = nm_ref[r0:r0 + nr, :]
        for q in range(e // LB):
            piece = x[:, q * LB:(q + 1) * LB] * gain * 0.5
            w_in_s[r0:r0 + nr, (2 * q + 1) * LB:(2 * q + 2) * LB] = piece.astype(BF16)

    @pl.when(_first_step())
    def _():
        chunks = [(w_grp_hbm.at[g], CB, CB, functools.partial(store_grp, g=g)) for g in range(n_grp)]
        for r0 in range(0, w_in_hbm.shape[0], STAGE_ROWS):
            nr = min(STAGE_ROWS, w_in_hbm.shape[0] - r0)
            chunks.append((w_in_hbm.at[pl.ds(r0, nr), pl.ds(0, e)], nr, e,
                           functools.partial(store_value, r0=r0, nr=nr)))
            chunks.append((w_in_hbm.at[pl.ds(r0, nr), pl.ds(e, e)], nr, e,
                           functools.partial(store_gate, r0=r0, nr=nr)))
        chunks += (_weight_chunks(w_out_hbm, w_out_s)
                   + _weight_chunks(wg_hbm, wg_s, part_scale=(0.5,), row_gain=pn_ref)
                   + _weight_chunks(wp_hbm, wp_s, part_scale=(0.5,)))
        _load_weights(chunks, stage_ref, sem_ref)

    s = pl.program_id(1)

    @pl.when(s == 0)
    def _():
        hist_ref[...] = jnp.zeros_like(hist_ref)

    def mixer_pre(r0, j, proj):
        window = POOL_WINDOWS[j]
        halves = []
        for hh in range(CB // LB):
            cs = slice(j * CB + hh * LB, j * CB + (hh + 1) * LB)
            v, hz = proj[:, 2 * hh * LB:(2 * hh + 1) * LB], proj[:, (2 * hh + 1) * LB:(2 * hh + 2) * LB]
            ext = jnp.concatenate([hist_ref[:, cs], v], axis=0)
            hist_ref[:, cs] = v[v.shape[0] - HIST:, :]
            span = 1
            while span < window:
                ext = ext + pltpu.roll(ext, span, axis=0)
                span *= 2
            win = ext[HIST:, :]
            row = lax.broadcasted_iota(jnp.int32, (HIST, LB), 0)
            t1 = (row + (s * TM + r0 + 1)).astype(F32)
            head = win[:HIST, :] * (1.0 / jnp.minimum(t1, float(window)))
            pooled = jnp.concatenate([head, win[HIST:, :] * (1.0 / window)], axis=0)
            halves.append(_half_silu(hz) * (pooled - v))
        return jnp.concatenate(halves, axis=1)

    _run_tile(h_ref, p_ref, fn_ref, o_ref, act_ref, w_in_s, w_out_s, wg_s, wp_s,
              POOL_SUBS, 2, mixer_pre)


def _layer_call(kernel_fn, h, p, layer, small_args, final_norm, weights, name):
    batch, seq, d = h.shape
    widx = tuple(w[1] for w in weights)
    wdtypes = [w[2] if len(w) > 2 else BF16 for w in weights]
    weights = [w[0] for w in weights]
    ple_dim = p.shape[-1]
    small = list(small_args) + ([final_norm] if final_norm is not None else [])
    const = lambda a: pl.BlockSpec(a.shape, lambda b, s: (0,) * a.ndim)
    in_specs = (
        [pl.BlockSpec((None, TM, d), lambda b, s: (b, s, 0)),
         pl.BlockSpec((None, None, TM, ple_dim), lambda b, s: (layer, b, s, 0))]
        + [const(a) for a in small]
        + [pl.BlockSpec(memory_space=pl.ANY)] * len(weights)
    )
    scratch = [pltpu.VMEM(w.shape[1:], dt) for w, dt in zip(weights, wdtypes)] + [
        pltpu.VMEM((TM, d), BF16),
        pltpu.VMEM((HIST, d), F32),
        pltpu.VMEM((STAGE_SLOTS, STAGE_ROWS, STAGE_COLS), F32),
        pltpu.SemaphoreType.DMA((STAGE_SLOTS,)),
    ]
    return pl.pallas_call(
        functools.partial(kernel_fn, final_norm is not None, widx),
        grid=(batch, seq // TM),
        in_specs=in_specs,
        out_specs=pl.BlockSpec((None, TM, d), lambda b, s: (b, s, 0)),
        out_shape=jax.ShapeDtypeStruct(h.shape, h.dtype),
        scratch_shapes=scratch,
        compiler_params=pltpu.CompilerParams(
            dimension_semantics=("arbitrary", "arbitrary"),
            vmem_limit_bytes=VMEM_LIMIT_BYTES,
        ),
        name=name,
    )(h, p, *small, *weights)


def kernel(x, p, norm_mix, a_w_in, a_w_conv, a_w_out, b_w_in, b_w_grp, b_scale, b_w_out, ple_norm,
           ple_w_gate, ple_w_proj, final_norm):
    depth = p.shape[0]
    assert x.shape[1] % TM == 0 and HIST >= max(POOL_WINDOWS) - 1
    for subs in (CONV_SUBS, POOL_SUBS):
        assert sum(subs) == TM and all(n % HIST == 0 for n in subs)
    row = lambda v: v.reshape(1, -1)
    col = lambda v: jnp.broadcast_to(v.reshape(-1, 1), (v.shape[0], LB))
    h = x
    for i in range(depth):
        j = i // 2
        fn = row(final_norm) if i == depth - 1 else None
        if i % 2 == 0:
            h = _layer_call(_conv_layer_kernel, h, p, i,
                            (col(norm_mix[i]), a_w_conv[j], col(ple_norm[i])), fn,
                            ((a_w_in, j), (a_w_out, j), (ple_w_gate, i), (ple_w_proj, i)),
                            f"conv_layer_{i}")
        else:
            h = _layer_call(_pool_layer_kernel, h, p, i,
                            (col(norm_mix[i]), row(b_scale[j]), col(ple_norm[i])), fn,
                            ((b_w_in, j), (b_w_grp, j, F32), (b_w_out, j), (ple_w_gate, i), (ple_w_proj, i)),
                            f"pool_layer_{i}")
    return h
```

```python
import functools

import jax
import jax.numpy as jnp
from jax import lax
from jax.experimental import pallas as pl
from jax.experimental.pallas import tpu as pltpu

EPS = 1e-6
POOL_WINDOWS = (2, 4, 8, 16)
HIST = 16
CB = 256
LB = 128
CONV_PART_ORDER = (1, 2, 0, 3)
TM = 1024
CONV_SUBS = (256, 256, 256, 256)
POOL_SUBS = (512, 512)
STAGE_ROWS = 512
STAGE_COLS = 1024
STAGE_SLOTS = 6
VMEM_LIMIT_BYTES = 56 * 1024 * 1024

F32 = jnp.float32
BF16 = jnp.bfloat16


def _normalize(x):
    ms = jnp.mean(x * x, axis=-1, keepdims=True)
    return x * lax.rsqrt(ms + EPS)


def _dot(a, b):
    return jnp.dot(a, b, preferred_element_type=F32)


def _dot_f32(a, b):
    a_hi, b_hi = a.astype(BF16), b.astype(BF16)
    a_lo = (a - a_hi.astype(F32)).astype(BF16)
    b_lo = (b - b_hi.astype(F32)).astype(BF16)
    return _dot(a_hi, b_hi) + (_dot(a_hi, b_lo) + _dot(a_lo, b_hi))


def _shift_rows(ext, k):
    return pltpu.roll(ext, k, axis=0)[HIST:, :]


def _half_silu(hz):
    return hz * (jnp.tanh(hz) + 1.0)


def _weight_chunks(src, dst, part_order=None, group=CB, part_scale=None, row_gain=None, col_gain=None):
    k, n = src.shape
    parts = 1 if part_order is None else len(part_order)
    order = (0,) if part_order is None else part_order
    e = n // parts
    chunks = []
    for r0 in range(0, k, STAGE_ROWS):
        nr = min(STAGE_ROWS, k - r0)
        for c0 in range(0, n, STAGE_COLS):
            nc = min(STAGE_COLS, n - c0)

            def store(x, r0=r0, nr=nr, c0=c0, nc=nc):
                if row_gain is not None:
                    gain = jnp.concatenate([row_gain[r0:r0 + nr, :]] * (group // LB), axis=1)
                for b0 in range(0, nc, group):
                    part, ch = divmod(c0 + b0, e)
                    piece = x[:, b0:b0 + group]
                    if row_gain is not None:
                        piece = piece * gain
                    if col_gain is not None:
                        piece = piece * col_gain[:, c0 + b0:c0 + b0 + group]
                    if part_scale is not None and part_scale[part] != 1.0:
                        piece = piece * part_scale[part]
                    d0 = (ch // group * parts + order.index(part)) * group
                    dst[r0:r0 + nr, d0:d0 + group] = piece.astype(BF16)

            chunks.append((src.at[pl.ds(r0, nr), pl.ds(c0, nc)], nr, nc, store))
    return chunks


def _load_weights(chunks, stage_ref, sem_ref):
    n_slots = stage_ref.shape[0]

    def copy(i):
        src, nr, nc, _ = chunks[i]
        slot = i % n_slots
        return pltpu.make_async_copy(src, stage_ref.at[slot, pl.ds(0, nr), pl.ds(0, nc)], sem_ref.at[slot])

    for i in range(min(n_slots - 1, len(chunks))):
        copy(i).start()
    for i, (_, nr, nc, store) in enumerate(chunks):
        if i + n_slots - 1 < len(chunks):
            copy(i + n_slots - 1).start()
        copy(i).wait()
        store(stage_ref[i % n_slots, 0:nr, 0:nc])


def _first_step():
    return jnp.logical_and(pl.program_id(0) == 0, pl.program_id(1) == 0)


def _ple_finish(h, half_pp, hn, wg_s, fn_ref):
    h = h + (jnp.tanh(_dot(hn, wg_s[...])) + 1.0) * half_pp
    if fn_ref is not None:
        h = _normalize(h) * fn_ref[...]
    return h


def _run_tile(h_ref, p_ref, fn_ref, o_ref, act_ref, w_in_s, w_out_s, wg_s, wp_s,
              subs, n_parts, mixer_pre, mixer_post=None):
    n_sub = len(subs)
    starts = [sum(subs[:k]) for k in range(n_sub)]
    n_blk = w_out_s.shape[0] // CB
    bw = n_parts * CB
    lag = 0 if mixer_post is None else 1
    n_units = n_sub * n_blk
    tail_start = {(k + 1) * n_blk + lag: k for k in range(n_sub)}
    sub = {}
    pending = None
    for t in range(n_units + lag + 2):
        if t < n_units:
            k, j = divmod(t, n_blk)
            rows = slice(starts[k], starts[k] + subs[k])
            if j == 0:
                h = h_ref[rows, :]
                sub[k] = {"rows": rows, "h": h, "hn": _normalize(h).astype(BF16)}
            if t == 0:
                sub[k]["half_pp"] = _dot(p_ref[rows, :].astype(BF16), wp_s[...])
            proj = _dot(sub[k]["hn"], w_in_s[:, j * bw:(j + 1) * bw])
        if pending is not None:
            pk, pj, state = pending
            act_ref[sub[pk]["rows"], pj * CB:(pj + 1) * CB] = mixer_post(pj, *state).astype(BF16)
            pending = None
        if t in tail_start:
            st = sub[tail_start[t]]
            st["h"] = st["h"] + _dot(act_ref[st["rows"], :], w_out_s[...])
            st["hn"] = _normalize(st["h"]).astype(BF16)
        if t - 1 in tail_start:
            st = sub.pop(tail_start[t - 1])
            if "half_pp" not in st:
                st["half_pp"] = _dot(p_ref[st["rows"], :].astype(BF16), wp_s[...])
            o_ref[st["rows"], :] = _ple_finish(st["h"], st["half_pp"], st["hn"], wg_s, fn_ref)
        if t < n_units:
            state = mixer_pre(starts[k], j, proj)
            if mixer_post is None:
                act_ref[rows, j * CB:(j + 1) * CB] = state.astype(BF16)
            else:
                pending = (k, j, state)


def _conv_layer_kernel(final_norm, widx, h_ref, p_ref, gain_ref, w_conv_ref, *rest):
    nm_ref, pn_ref = gain_ref.at[0], gain_ref.at[1]
    if final_norm:
        fn_ref, *rest = rest
    else:
        fn_ref = None
    (w_in_hbm, w_out_hbm, wg_hbm, wp_hbm, o_ref,
     w_in_s, w_out_s, wg_s, wp_s, act_ref, hist_ref, stage_ref, sem_ref) = rest
    w_in_hbm, w_out_hbm, wg_hbm, wp_hbm = (
        r.at[i] for r, i in zip((w_in_hbm, w_out_hbm, wg_hbm, wp_hbm), widx))

    @pl.when(_first_step())
    def _():
        chunks = (_weight_chunks(w_in_hbm, w_in_s, CONV_PART_ORDER, LB, (1.0, 1.0, 1.0, 0.5), nm_ref)
                  + _weight_chunks(w_out_hbm, w_out_s)
                  + _weight_chunks(wg_hbm, wg_s, part_scale=(0.5,), row_gain=pn_ref)
                  + _weight_chunks(wp_hbm, wp_s, part_scale=(0.5,)))
        _load_weights(chunks, stage_ref, sem_ref)

    @pl.when(pl.program_id(1) == 0)
    def _():
        hist_ref[...] = jnp.zeros_like(hist_ref)

    def mixer_pre(r0, j, proj):
        halves = []
        for hh in range(CB // LB):
            cs = slice(j * CB + hh * LB, j * CB + (hh + 1) * LB)
            c_g, v, b_g, hz = (proj[:, (4 * hh + q) * LB:(4 * hh + q + 1) * LB] for q in range(4))
            u = c_g * v
            ext = jnp.concatenate([hist_ref[:, cs], u], axis=0)
            hist_ref[:, cs] = u[u.shape[0] - HIST:, :]
            w = w_conv_ref[:, cs]
            y = _shift_rows(ext, 2) * w[0:1] + _shift_rows(ext, 1) * w[1:2] + u * w[2:3]
            halves.append(_half_silu(hz) * (b_g * y))
        return jnp.concatenate(halves, axis=1)

    _run_tile(h_ref, p_ref, fn_ref, o_ref, act_ref, w_in_s, w_out_s, wg_s, wp_s,
              CONV_SUBS, 4, mixer_pre)


def _pool_layer_kernel(final_norm, widx, h_ref, p_ref, gain_ref, scale_ref, *rest):
    nm_ref, pn_ref = gain_ref.at[0], gain_ref.at[1]
    if final_norm:
        fn_ref, *rest = rest
    else:
        fn_ref = None
    (w_in_hbm, w_grp_hbm, w_out_hbm, wg_hbm, wp_hbm, o_ref,
     w_in_s, w_grp_s, w_out_s, wg_s, wp_s, act_ref, hist_ref, stage_ref, sem_ref) = rest
    w_in_hbm, w_grp_hbm, w_out_hbm, wg_hbm, wp_hbm = (
        r.at[i] for r, i in zip((w_in_hbm, w_grp_hbm, w_out_hbm, wg_hbm, wp_hbm), widx))

    n_grp = len(POOL_WINDOWS)
    e = n_grp * CB

    def store_grp(x, g):
        w_grp_s[g] = x[...] * scale_ref[widx[1]:widx[1] + 1, g * CB:(g + 1) * CB]

    def store_value(x, r0, nr):
        gain = jnp.concatenate([nm_ref[r0:r0 + nr, :]] * (CB // LB), axis=1)
        for g in range(n_grp):
            fused = _dot_f32(x[:, g * CB:(g + 1) * CB] * gain, w_grp_s[g])
            for hh in range(CB // LB):
                q = g * (CB // LB) + hh
                w_in_s[r0:r0 + nr, 2 * q * LB:(2 * q + 1) * LB] = fused[:, hh * LB:(hh + 1) * LB].astype(BF16)

    def store_gate(x, r0, nr):
        gain = nm_ref[r0:r0 + nr, :]
        for q in range(e // LB):
            piece = x[:, q * LB:(q + 1) * LB] * gain * 0.5
            w_in_s[r0:r0 + nr, (2 * q + 1) * LB:(2 * q + 2) * LB] = piece.astype(BF16)

    @pl.when(_first_step())
    def _():
        chunks = [(w_grp_hbm.at[g], CB, CB, functools.partial(store_grp, g=g)) for g in range(n_grp)]
        for r0 in range(0, w_in_hbm.shape[0], STAGE_ROWS):
            nr = min(STAGE_ROWS, w_in_hbm.shape[0] - r0)
            chunks.append((w_in_hbm.at[pl.ds(r0, nr), pl.ds(0, e)], nr, e,
                           functools.partial(store_value, r0=r0, nr=nr)))
            chunks.append((w_in_hbm.at[pl.ds(r0, nr), pl.ds(e, e)], nr, e,
                           functools.partial(store_gate, r0=r0, nr=nr)))
        chunks += (_weight_chunks(w_out_hbm, w_out_s)
                   + _weight_chunks(wg_hbm, wg_s, part_scale=(0.5,), row_gain=pn_ref)
                   + _weight_chunks(wp_hbm, wp_s, part_scale=(0.5,)))
        _load_weights(chunks, stage_ref, sem_ref)

    s = pl.program_id(1)

    @pl.when(s == 0)
    def _():
        hist_ref[...] = jnp.zeros_like(hist_ref)

    def mixer_pre(r0, j, proj):
        window = POOL_WINDOWS[j]
        halves = []
        for hh in range(CB // LB):
            cs = slice(j * CB + hh * LB, j * CB + (hh + 1) * LB)
            v, hz = proj[:, 2 * hh * LB:(2 * hh + 1) * LB], proj[:, (2 * hh + 1) * LB:(2 * hh + 2) * LB]
            ext = jnp.concatenate([hist_ref[:, cs], v], axis=0)
            hist_ref[:, cs] = v[v.shape[0] - HIST:, :]
            span = 1
            while span < window:
                ext = ext + pltpu.roll(ext, span, axis=0)
                span *= 2
            win = ext[HIST:, :]
            row = lax.broadcasted_iota(jnp.int32, (HIST, LB), 0)
            t1 = (row + (s * TM + r0 + 1)).astype(F32)
            head = win[:HIST, :] * (1.0 / jnp.minimum(t1, float(window)))
            pooled = jnp.concatenate([head, win[HIST:, :] * (1.0 / window)], axis=0)
            halves.append(_half_silu(hz) * (pooled - v))
        return jnp.concatenate(halves, axis=1)

    _run_tile(h_ref, p_ref, fn_ref, o_ref, act_ref, w_in_s, w_out_s, wg_s, wp_s,
              POOL_SUBS, 2, mixer_pre)


def _layer_call(kernel_fn, h, p, layer, small_args, final_norm, weights, name):
    batch, seq, d = h.shape
    widx = tuple(w[1] for w in weights)
    wdtypes = [w[2] if len(w) > 2 else BF16 for w in weights]
    weights = [w[0] for w in weights]
    ple_dim = p.shape[-1]
    small_idx = list(small_args) + ([(final_norm, 0)] if final_norm is not None else [])
    small = [a for a, _ in small_idx]

    def sliced(a, idx):
        if idx is None:
            return pl.BlockSpec(a.shape, lambda b, s: (0,) * a.ndim)
        return pl.BlockSpec((None,) + a.shape[1:], lambda b, s: (idx,) + (0,) * (a.ndim - 1))

    in_specs = (
        [pl.BlockSpec((None, TM, d), lambda b, s: (b, s, 0)),
         pl.BlockSpec((None, None, TM, ple_dim), lambda b, s: (layer, b, s, 0))]
        + [sliced(a, idx) for a, idx in small_idx]
        + [pl.BlockSpec(memory_space=pl.ANY)] * len(weights)
    )
    scratch = [pltpu.VMEM(w.shape[1:], dt) for w, dt in zip(weights, wdtypes)] + [
        pltpu.VMEM((TM, d), BF16),
        pltpu.VMEM((HIST, d), F32),
        pltpu.VMEM((STAGE_SLOTS, STAGE_ROWS, STAGE_COLS), F32),
        pltpu.SemaphoreType.DMA((STAGE_SLOTS,)),
    ]
    return pl.pallas_call(
        functools.partial(kernel_fn, final_norm is not None, widx),
        grid=(batch, seq // TM),
        in_specs=in_specs,
        out_specs=pl.BlockSpec((None, TM, d), lambda b, s: (b, s, 0)),
        out_shape=jax.ShapeDtypeStruct(h.shape, h.dtype),
        scratch_shapes=scratch,
        compiler_params=pltpu.CompilerParams(
            dimension_semantics=("arbitrary", "arbitrary"),
            vmem_limit_bytes=VMEM_LIMIT_BYTES,
        ),
        name=name,
    )(h, p, *small, *weights)


def kernel(x, p, norm_mix, a_w_in, a_w_conv, a_w_out, b_w_in, b_w_grp, b_scale, b_w_out, ple_norm,
           ple_w_gate, ple_w_proj, final_norm):
    depth = p.shape[0]
    assert x.shape[1] % TM == 0 and HIST >= max(POOL_WINDOWS) - 1
    for subs in (CONV_SUBS, POOL_SUBS):
        assert sum(subs) == TM and all(n % HIST == 0 for n in subs)
    d = x.shape[2]
    gains = jnp.broadcast_to(jnp.stack([norm_mix, ple_norm], axis=1)[..., None], (depth, 2, d, LB))
    h = x
    for i in range(depth):
        j = i // 2
        fn = final_norm.reshape(1, 1, d) if i == depth - 1 else None
        if i % 2 == 0:
            h = _layer_call(_conv_layer_kernel, h, p, i,
                            ((gains, i), (a_w_conv, j)), fn,
                            ((a_w_in, j), (a_w_out, j), (ple_w_gate, i), (ple_w_proj, i)),
                            f"conv_layer_{i}")
        else:
            h = _layer_call(_pool_layer_kernel, h, p, i,
                            ((gains, i), (b_scale, None)), fn,
                            ((b_w_in, j), (b_w_grp, j, F32), (b_w_out, j), (ple_w_gate, i), (ple_w_proj, i)),
                            f"pool_layer_{i}")
    return h
```

```python
import functools

import jax
import jax.numpy as jnp
from jax import lax
from jax.experimental import pallas as pl
from jax.experimental.pallas import tpu as pltpu

EPS = 1e-6
POOL_WINDOWS = (2, 4, 8, 16)
HIST = 16
CB = 256
LB = 128
CONV_PART_ORDER = (1, 2, 0, 3)
TM = 1024
CONV_SUBS = (256, 256, 256, 256)
POOL_SUBS = (512, 512)
STAGE_ROWS = 512
STAGE_COLS = 1024
STAGE_SLOTS = 4
VMEM_LIMIT_BYTES = 56 * 1024 * 1024

F32 = jnp.float32
BF16 = jnp.bfloat16


def _normalize(x):
    ms = jnp.mean(x * x, axis=-1, keepdims=True)
    return x * lax.rsqrt(ms + EPS)


def _dot(a, b):
    return jnp.dot(a, b, preferred_element_type=F32)


def _dot_f32(a, b):
    a_hi, b_hi = a.astype(BF16), b.astype(BF16)
    a_lo = (a - a_hi.astype(F32)).astype(BF16)
    b_lo = (b - b_hi.astype(F32)).astype(BF16)
    return _dot(a_hi, b_hi) + (_dot(a_hi, b_lo) + _dot(a_lo, b_hi))


def _shift_rows(ext, k):
    return pltpu.roll(ext, k, axis=0)[HIST:, :]


def _half_silu(hz):
    return hz * (jnp.tanh(hz) + 1.0)


def _weight_chunks(src, dst, part_order=None, group=CB, part_scale=None, row_gain=None, col_gain=None):
    k, n = src.shape
    parts = 1 if part_order is None else len(part_order)
    order = (0,) if part_order is None else part_order
    e = n // parts
    chunks = []
    for r0 in range(0, k, STAGE_ROWS):
        nr = min(STAGE_ROWS, k - r0)
        for c0 in range(0, n, STAGE_COLS):
            nc = min(STAGE_COLS, n - c0)

            def store(x, r0=r0, nr=nr, c0=c0, nc=nc):
                if row_gain is not None:
                    gain = jnp.concatenate([row_gain[r0:r0 + nr, :]] * (group // LB), axis=1)
                for b0 in range(0, nc, group):
                    part, ch = divmod(c0 + b0, e)
                    piece = x[:, b0:b0 + group]
                    if row_gain is not None:
                        piece = piece * gain
                    if col_gain is not None:
                        piece = piece * col_gain[:, c0 + b0:c0 + b0 + group]
                    if part_scale is not None and part_scale[part] != 1.0:
                        piece = piece * part_scale[part]
                    d0 = (ch // group * parts + order.index(part)) * group
                    dst[r0:r0 + nr, d0:d0 + group] = piece.astype(BF16)

            chunks.append((src.at[pl.ds(r0, nr), pl.ds(c0, nc)], nr, nc, store))
    return chunks


def _load_weights(chunks, stage_ref, sem_ref):
    n_slots = stage_ref.shape[0]

    def copy(i):
        src, nr, nc, _ = chunks[i]
        slot = i % n_slots
        return pltpu.make_async_copy(src, stage_ref.at[slot, pl.ds(0, nr), pl.ds(0, nc)], sem_ref.at[slot])

    for i in range(min(n_slots - 1, len(chunks))):
        copy(i).start()
    for i, (_, nr, nc, store) in enumerate(chunks):
        if i + n_slots - 1 < len(chunks):
            copy(i + n_slots - 1).start()
        copy(i).wait()
        store(stage_ref[i % n_slots, 0:nr, 0:nc])


def _first_step():
    return jnp.logical_and(pl.program_id(0) == 0, pl.program_id(1) == 0)


def _ple_finish(h, half_pp, hn, wg_s, fn_ref):
    h = h + (jnp.tanh(_dot(hn, wg_s[...])) + 1.0) * half_pp
    if fn_ref is not None:
        h = _normalize(h) * fn_ref[...]
    return h


def _run_tile(h_ref, p_ref, fn_ref, o_ref, act_ref, w_in_s, w_out_s, wg_s, wp_s,
              subs, n_parts, mixer_pre, mixer_post=None):
    n_sub = len(subs)
    starts = [sum(subs[:k]) for k in range(n_sub)]
    n_blk = w_out_s.shape[0] // CB
    bw = n_parts * CB
    lag = 0 if mixer_post is None else 1
    n_units = n_sub * n_blk
    tail_start = {(k + 1) * n_blk + lag: k for k in range(n_sub)}
    sub = {}
    pending = None
    for t in range(n_units + lag + 2):
        if t < n_units:
            k, j = divmod(t, n_blk)
            rows = slice(starts[k], starts[k] + subs[k])
            if j == 0:
                h = h_ref[rows, :]
                sub[k] = {"rows": rows, "h": h, "hn": _normalize(h).astype(BF16)}
            if t == 0:
                sub[k]["half_pp"] = _dot(p_ref[rows, :].astype(BF16), wp_s[...])
            proj = _dot(sub[k]["hn"], w_in_s[:, j * bw:(j + 1) * bw])
        if pending is not None:
            pk, pj, state = pending
            act_ref[sub[pk]["rows"], pj * CB:(pj + 1) * CB] = mixer_post(pj, *state).astype(BF16)
            pending = None
        if t in tail_start:
            st = sub[tail_start[t]]
            st["h"] = st["h"] + _dot(act_ref[st["rows"], :], w_out_s[...])
            st["hn"] = _normalize(st["h"]).astype(BF16)
        if t - 1 in tail_start:
            st = sub.pop(tail_start[t - 1])
            if "half_pp" not in st:
                st["half_pp"] = _dot(p_ref[st["rows"], :].astype(BF16), wp_s[...])
            o_ref[st["rows"], :] = _ple_finish(st["h"], st["half_pp"], st["hn"], wg_s, fn_ref)
        if t < n_units:
            state = mixer_pre(starts[k], j, proj)
            if mixer_post is None:
                act_ref[rows, j * CB:(j + 1) * CB] = state.astype(BF16)
            else:
                pending = (k, j, state)


def _conv_layer_kernel(final_norm, widx, h_ref, p_ref, gain_ref, w_conv_ref, *rest):
    nm_ref, pn_ref = gain_ref.at[0], gain_ref.at[1]
    if final_norm:
        fn_ref, *rest = rest
    else:
        fn_ref = None
    (w_in_hbm, w_out_hbm, wg_hbm, wp_hbm, o_ref,
     w_in_s, w_out_s, wg_s, wp_s, act_ref, hist_ref, stage_ref, sem_ref) = rest
    w_in_hbm, w_out_hbm, wg_hbm, wp_hbm = (
        r.at[i] for r, i in zip((w_in_hbm, w_out_hbm, wg_hbm, wp_hbm), widx))

    @pl.when(_first_step())
    def _():
        chunks = (_weight_chunks(w_in_hbm, w_in_s, CONV_PART_ORDER, LB, (1.0, 1.0, 1.0, 0.5), nm_ref)
                  + _weight_chunks(w_out_hbm, w_out_s)
                  + _weight_chunks(wg_hbm, wg_s, part_scale=(0.5,), row_gain=pn_ref)
                  + _weight_chunks(wp_hbm, wp_s, part_scale=(0.5,)))
        _load_weights(chunks, stage_ref, sem_ref)

    @pl.when(pl.program_id(1) == 0)
    def _():
        hist_ref[...] = jnp.zeros_like(hist_ref)

    def mixer_pre(r0, j, proj):
        halves = []
        for hh in range(CB // LB):
            cs = slice(j * CB + hh * LB, j * CB + (hh + 1) * LB)
            c_g, v, b_g, hz = (proj[:, (4 * hh + q) * LB:(4 * hh + q + 1) * LB] for q in range(4))
            u = c_g * v
            ext = jnp.concatenate([hist_ref[:, cs], u], axis=0)
            hist_ref[:, cs] = u[u.shape[0] - HIST:, :]
            w = w_conv_ref[:, cs]
            y = _shift_rows(ext, 2) * w[0:1] + _shift_rows(ext, 1) * w[1:2] + u * w[2:3]
            halves.append(_half_silu(hz) * (b_g * y))
        return jnp.concatenate(halves, axis=1)

    _run_tile(h_ref, p_ref, fn_ref, o_ref, act_ref, w_in_s, w_out_s, wg_s, wp_s,
              CONV_SUBS, 4, mixer_pre)


def _pool_layer_kernel(final_norm, widx, h_ref, p_ref, gain_ref, scale_ref, *rest):
    nm_ref, pn_ref = gain_ref.at[0], gain_ref.at[1]
    if final_norm:
        fn_ref, *rest = rest
    else:
        fn_ref = None
    (w_in_hbm, w_grp_hbm, w_out_hbm, wg_hbm, wp_hbm, o_ref,
     w_in_s, w_grp_s, w_out_s, wg_s, wp_s, act_ref, hist_ref, stage_ref, sem_ref) = rest
    w_in_hbm, w_grp_hbm, w_out_hbm, wg_hbm, wp_hbm = (
        r.at[i] for r, i in zip((w_in_hbm, w_grp_hbm, w_out_hbm, wg_hbm, wp_hbm), widx))

    n_grp = len(POOL_WINDOWS)
    e = n_grp * CB

    def store_grp(x, g):
        w_grp_s[g] = x[...] * scale_ref[widx[1]:widx[1] + 1, g * CB:(g + 1) * CB]

    def store_value(x, r0, nr):
        gain = jnp.concatenate([nm_ref[r0:r0 + nr, :]] * (CB // LB), axis=1)
        for g in range(n_grp):
            fused = _dot_f32(x[:, g * CB:(g + 1) * CB] * gain, w_grp_s[g]) * (1.0 / POOL_WINDOWS[g])
            for hh in range(CB // LB):
                q = g * (CB // LB) + hh
                w_in_s[r0:r0 + nr, 2 * q * LB:(2 * q + 1) * LB] = fused[:, hh * LB:(hh + 1) * LB].astype(BF16)

    def store_gate(x, r0, nr):
        gain = nm_ref[r0:r0 + nr, :]
        for q in range(e // LB):
            piece = x[:, q * LB:(q + 1) * LB] * gain * 0.5
            w_in_s[r0:r0 + nr, (2 * q + 1) * LB:(2 * q + 2) * LB] = piece.astype(BF16)

    @pl.when(_first_step())
    def _():
        chunks = [(w_grp_hbm.at[g], CB, CB, functools.partial(store_grp, g=g)) for g in range(n_grp)]
        for r0 in range(0, w_in_hbm.shape[0], STAGE_ROWS):
            nr = min(STAGE_ROWS, w_in_hbm.shape[0] - r0)
            chunks.append((w_in_hbm.at[pl.ds(r0, nr), pl.ds(0, e)], nr, e,
                           functools.partial(store_value, r0=r0, nr=nr)))
            chunks.append((w_in_hbm.at[pl.ds(r0, nr), pl.ds(e, e)], nr, e,
                           functools.partial(store_gate, r0=r0, nr=nr)))
        chunks += (_weight_chunks(w_out_hbm, w_out_s)
                   + _weight_chunks(wg_hbm, wg_s, part_scale=(0.5,), row_gain=pn_ref)
                   + _weight_chunks(wp_hbm, wp_s, part_scale=(0.5,)))
        _load_weights(chunks, stage_ref, sem_ref)

    s = pl.program_id(1)

    @pl.when(s == 0)
    def _():
        hist_ref[...] = jnp.zeros_like(hist_ref)

    def mixer_pre(r0, j, proj):
        window = POOL_WINDOWS[j]
        halves = []
        for hh in range(CB // LB):
            cs = slice(j * CB + hh * LB, j * CB + (hh + 1) * LB)
            v, hz = proj[:, 2 * hh * LB:(2 * hh + 1) * LB], proj[:, (2 * hh + 1) * LB:(2 * hh + 2) * LB]
            ext = jnp.concatenate([hist_ref[:, cs], v], axis=0)
            hist_ref[:, cs] = v[v.shape[0] - HIST:, :]
            row = lax.broadcasted_iota(jnp.int32, (HIST, LB), 0)
            t1 = (row + (s * TM + r0 + 1)).astype(F32)
            head_scale = float(window) / jnp.minimum(t1, float(window))
            if window == 2:
                prev = pltpu.roll(ext, 1, axis=0)[HIST:, :]
                head = (prev[:HIST, :] + v[:HIST, :]) * head_scale - 2.0 * v[:HIST, :]
                body = prev[HIST:, :] - v[HIST:, :]
            else:
                span = 1
                while span < window:
                    ext = ext + pltpu.roll(ext, span, axis=0)
                    span *= 2
                win = ext[HIST:, :]
                head = win[:HIST, :] * head_scale - float(window) * v[:HIST, :]
                body = win[HIST:, :] - float(window) * v[HIST:, :]
            halves.append(_half_silu(hz) * jnp.concatenate([head, body], axis=0))
        return jnp.concatenate(halves, axis=1)

    _run_tile(h_ref, p_ref, fn_ref, o_ref, act_ref, w_in_s, w_out_s, wg_s, wp_s,
              POOL_SUBS, 2, mixer_pre)


def _layer_call(kernel_fn, h, p, layer, small_args, final_norm, weights, name):
    batch, seq, d = h.shape
    widx = tuple(w[1] for w in weights)
    wdtypes = [w[2] if len(w) > 2 else BF16 for w in weights]
    weights = [w[0] for w in weights]
    ple_dim = p.shape[-1]
    small_idx = list(small_args) + ([(final_norm, 0)] if final_norm is not None else [])
    small = [a for a, _ in small_idx]

    def sliced(a, idx):
        if idx is None:
            return pl.BlockSpec(a.shape, lambda b, s: (0,) * a.ndim)
        return pl.BlockSpec((None,) + a.shape[1:], lambda b, s: (idx,) + (0,) * (a.ndim - 1))

    in_specs = (
        [pl.BlockSpec((None, TM, d), lambda b, s: (b, s, 0)),
         pl.BlockSpec((None, None, TM, ple_dim), lambda b, s: (layer, b, s, 0))]
        + [sliced(a, idx) for a, idx in small_idx]
        + [pl.BlockSpec(memory_space=pl.ANY)] * len(weights)
    )
    scratch = [pltpu.VMEM(w.shape[1:], dt) for w, dt in zip(weights, wdtypes)] + [
        pltpu.VMEM((TM, d), BF16),
        pltpu.VMEM((HIST, d), F32),
        pltpu.VMEM((STAGE_SLOTS, STAGE_ROWS, STAGE_COLS), F32),
        pltpu.SemaphoreType.DMA((STAGE_SLOTS,)),
    ]
    return pl.pallas_call(
        functools.partial(kernel_fn, final_norm is not None, widx),
        grid=(batch, seq // TM),
        in_specs=in_specs,
        out_specs=pl.BlockSpec((None, TM, d), lambda b, s: (b, s, 0)),
        out_shape=jax.ShapeDtypeStruct(h.shape, h.dtype),
        scratch_shapes=scratch,
        compiler_params=pltpu.CompilerParams(
            dimension_semantics=("arbitrary", "arbitrary"),
            vmem_limit_bytes=VMEM_LIMIT_BYTES,
        ),
        name=name,
    )(h, p, *small, *weights)


def kernel(x, p, norm_mix, a_w_in, a_w_conv, a_w_out, b_w_in, b_w_grp, b_scale, b_w_out, ple_norm,
           ple_w_gate, ple_w_proj, final_norm):
    depth = p.shape[0]
    assert x.shape[1] % TM == 0 and HIST >= max(POOL_WINDOWS) - 1
    for subs in (CONV_SUBS, POOL_SUBS):
        assert sum(subs) == TM and all(n % HIST == 0 for n in subs)
    d = x.shape[2]
    gains = jnp.broadcast_to(jnp.stack([norm_mix, ple_norm], axis=1)[..., None], (depth, 2, d, LB))
    h = x
    for i in range(depth):
        j = i // 2
        fn = final_norm.reshape(1, 1, d) if i == depth - 1 else None
        if i % 2 == 0:
            h = _layer_call(_conv_layer_kernel, h, p, i,
                            ((gains, i), (a_w_conv, j)), fn,
                            ((a_w_in, j), (a_w_out, j), (ple_w_gate, i), (ple_w_proj, i)),
                            f"conv_layer_{i}")
        else:
            h = _layer_call(_pool_layer_kernel, h, p, i,
                            ((gains, i), (b_scale, None)), fn,
                            ((b_w_in, j), (b_w_grp, j, F32), (b_w_out, j), (ple_w_gate, i), (ple_w_proj, i)),
                            f"pool_layer_{i}")
    return h
```
